```python
import math, functools
import jax, jax.numpy as jnp
from jax import lax
import numpy as np

D_MODEL = 1024
BATCH = 8
SEQ = 8192
DEPTH = 1
DEC_BATCH = 16
DEC_SEQ = 32
PAST_LEN = 2048

CHUNK = 64
N_META = 16
RWKV_HEAD = 64
RWKV_HEADS = D_MODEL // RWKV_HEAD
RWKV_DIM = RWKV_HEADS * RWKV_HEAD
DECAY_LORA = 64
AAA_LORA = 64
GATE_LORA = 128
RWKV_PROJ = 3 * RWKV_DIM + DECAY_LORA + AAA_LORA + GATE_LORA
LNX_EPS = 64e-5
ATT_HEADS = 8
QK_DIM = 64
V_DIM = 2 * QK_DIM
ATT_QK = ATT_HEADS * 2 * QK_DIM
ATT_DIM = ATT_HEADS * V_DIM
ATT_PROJ = 2 * ATT_QK + ATT_DIM
GATE_PROJ = 2 * D_MODEL
IN_PROJ = RWKV_PROJ + ATT_PROJ + GATE_PROJ
Q_BLOCK = 128
N_EXPERTS = 32
TOP_K = 4
D_FF = D_MODEL
SWIGLU_ALPHA = 1.702
SWIGLU_LIMIT = 7.0
MOE_BLOCK = 256
RMS_EPS = 1e-6
NEG_INF = -1e30

kernel_name = 'hybrid_rwkv7_diffattn_moe_stream_step'


def alibi_slopes(n_heads):
    return np.array([2.0 ** (-8.0 * (i + 1) / n_heads) for i in range(n_heads)], dtype=np.float32)


def lambda_init(layer):
    return 0.8 - 0.6 * math.exp(-0.3 * layer)


def rms_norm(x, g):
    xf = x.astype(jnp.float32)
    return xf * lax.rsqrt(jnp.mean(xf * xf, axis=-1, keepdims=True) + RMS_EPS) * g.astype(jnp.float32)


def prompt_chunk_ids(pos):
    return jnp.where(pos < N_META, -1, (pos - N_META) // CHUNK)


def rwkv7_mix(pr, prev_row, s0, mu, w0, w_up, a0, a_up, g_up, k_k, k_a, r_k, lnx_g, lnx_b):
    b, L, _ = pr.shape
    f32 = jnp.float32
    prf = pr.astype(f32)
    prev = jnp.concatenate([prev_row.astype(f32), prf[:, :-1]], axis=1)
    xs = prf + (prev - prf) * mu.astype(f32)
    r, k, v, wd, ad, gd = jnp.split(xs, [RWKV_DIM, 2 * RWKV_DIM, 3 * RWKV_DIM,
                                         3 * RWKV_DIM + DECAY_LORA,
                                         3 * RWKV_DIM + DECAY_LORA + AAA_LORA], axis=-1)
    w = -jax.nn.softplus(-(w0 + jnp.tanh(wd) @ w_up)) - 0.5
    decay = jnp.exp(-jnp.exp(w))
    a = jax.nn.sigmoid(a0 + ad @ a_up)
    g = jax.nn.sigmoid(gd) @ g_up
    hd = lambda t: t.reshape(b, L, RWKV_HEADS, RWKV_HEAD)
    kk = hd(k * k_k)
    kk = kk / jnp.maximum(jnp.linalg.norm(kk, axis=-1, keepdims=True), 1e-12)
    k = k * (1.0 + (a - 1.0) * k_a)
    r_h, k_h, v_h, a_h, d_h = hd(r), hd(k), hd(v), hd(a), hd(decay)

    def step(S, inp):
        r_t, d_t, k_t, v_t, kk_t, a_t = inp
        sa = jnp.einsum('bhvk,bhk->bhv', S, -kk_t)
        S = (S * d_t[:, :, None, :] + sa[..., None] * (kk_t * a_t)[:, :, None, :]
             + v_t[..., None] * k_t[:, :, None, :])
        return S, jnp.einsum('bhvk,bhk->bhv', S, r_t)

    seq = tuple(jnp.moveaxis(t, 1, 0) for t in (r_h, d_h, k_h, v_h, kk, a_h))
    s_fin, y = lax.scan(step, s0.astype(f32), seq)
    y = jnp.moveaxis(y, 0, 1)
    mean = jnp.mean(y, axis=-1, keepdims=True)
    var = jnp.mean(jnp.square(y - mean), axis=-1, keepdims=True)
    y = ((y - mean) * lax.rsqrt(var + LNX_EPS)).reshape(b, L, RWKV_DIM) * lnx_g + lnx_b
    bonus = jnp.sum(r_h * k_h * r_k, axis=-1, keepdims=True) * v_h
    y = (y + bonus.reshape(b, L, RWKV_DIM)) * g
    return y, s_fin, pr[:, -1:]


def diff_attn_qkv(pa, q_norm_g, k_norm_g):
    b, L, _ = pa.shape
    q, k, v = jnp.split(pa, [ATT_QK, 2 * ATT_QK], axis=-1)
    q = rms_norm(q.reshape(b, L, ATT_HEADS, 2, QK_DIM), q_norm_g)
    k = rms_norm(k.reshape(b, L, ATT_HEADS, 2, QK_DIM), k_norm_g)
    v = v.reshape(b, L, ATT_HEADS, V_DIM).astype(jnp.float32)
    return q, k, v


def diff_attn_core(q, q_pos, q_chunk, k, v, k_pos, k_chunk, lam, slopes):
    s = jnp.einsum('bqhmd,bkhmd->bhmqk', q, k) * (QK_DIM ** -0.5)
    dist = jnp.abs(q_pos[:, None] - k_pos[None, :]).astype(jnp.float32)
    s = s - slopes[None, :, None, None, None] * dist
    allowed = k_chunk[None, :] <= q_chunk[:, None]
    s = jnp.where(allowed, s, NEG_INF)
    p = jax.nn.softmax(s, axis=-1)
    attn = p[:, :, 0] - lam * p[:, :, 1]
    return jnp.einsum('bhqk,bkhv->bqhv', attn, v)


def prompt_attention(q, k, v, lam, slopes):
    b, L = q.shape[0], q.shape[1]
    n_blk = -(-L // Q_BLOCK)
    Lp = n_blk * Q_BLOCK
    pos = jnp.arange(Lp, dtype=jnp.int32)
    chunk = prompt_chunk_ids(pos)
    q_pad = jnp.pad(q, ((0, 0), (0, Lp - L), (0, 0), (0, 0), (0, 0)))
    q_blocks = jnp.moveaxis(q_pad.reshape(b, n_blk, Q_BLOCK, ATT_HEADS, 2, QK_DIM), 1, 0)
    k_pos, k_chunk = pos[:L], chunk[:L]

    def one_block(args):
        q_b, p_b, c_b = args
        return diff_attn_core(q_b, p_b, c_b, k, v, k_pos, k_chunk, lam, slopes)

    o = lax.map(one_block, (q_blocks, pos.reshape(n_blk, Q_BLOCK), chunk.reshape(n_blk, Q_BLOCK)))
    return jnp.moveaxis(o, 0, 1).reshape(b, Lp, ATT_HEADS, V_DIM)[:, :L]


def sample_attention(q, k, v, lam, slopes, cache_k, cache_v):
    past = cache_k.shape[1]
    n_new = q.shape[1]
    k_all = jnp.concatenate([cache_k.astype(jnp.float32), k], axis=1)
    v_all = jnp.concatenate([cache_v.astype(jnp.float32), v], axis=1)
    pos = jnp.arange(past + n_new, dtype=jnp.int32)
    chunk = pos // CHUNK
    return diff_attn_core(q, pos[past:], chunk[past:], k_all, v_all, pos, chunk, lam, slopes)


def moe_ffn(h, router_w, router_b, w_gu, b_gu, w_dn, b_dn):
    T, D = h.shape
    logits = h.astype(jnp.float32) @ router_w.astype(jnp.float32) + router_b.astype(jnp.float32)
    top_v, top_e = lax.top_k(logits, TOP_K)
    gate = jax.nn.softmax(top_v, axis=-1)
    rows = T * TOP_K
    flat_e = top_e.reshape(-1)
    order = jnp.argsort(flat_e)
    e_sorted = flat_e[order]
    tok_sorted = (order // TOP_K).astype(jnp.int32)
    gate_sorted = gate.reshape(-1)[order]
    counts = jnp.bincount(flat_e, length=N_EXPERTS)
    padded = (counts + MOE_BLOCK - 1) // MOE_BLOCK * MOE_BLOCK
    pad_end = jnp.cumsum(padded)
    pad_start = pad_end - padded
    start = jnp.cumsum(counts) - counts
    dest = pad_start[e_sorted] + jnp.arange(rows, dtype=jnp.int32) - start[e_sorted]
    n_blk = -(-(rows + N_EXPERTS * (MOE_BLOCK - 1)) // MOE_BLOCK)
    n_rows = n_blk * MOE_BLOCK
    row_tok = jnp.full((n_rows,), T, jnp.int32).at[dest].set(tok_sorted)
    row_gate = jnp.zeros((n_rows,), jnp.float32).at[dest].set(gate_sorted)
    blk_start = jnp.arange(n_blk, dtype=pad_end.dtype) * MOE_BLOCK
    blk_e = jnp.minimum(jnp.searchsorted(pad_end, blk_start, side='right'), N_EXPERTS - 1)
    h_pad = jnp.concatenate([h, jnp.zeros((1, D), h.dtype)], axis=0)

    def expert_block(args):
        tok, g, e = args
        gu = h_pad[tok] @ w_gu[e] + b_gu[e]
        x_glu, x_lin = jnp.split(gu, 2, axis=-1)
        x_glu = jnp.minimum(x_glu, SWIGLU_LIMIT)
        x_lin = jnp.clip(x_lin, -SWIGLU_LIMIT, SWIGLU_LIMIT)
        act = x_glu * jax.nn.sigmoid(SWIGLU_ALPHA * x_glu) * (x_lin + 1.0)
        out = act @ w_dn[e] + b_dn[e]
        return out * g[:, None].astype(out.dtype)

    out = lax.map(expert_block, (row_tok.reshape(n_blk, MOE_BLOCK),
                                 row_gate.reshape(n_blk, MOE_BLOCK), blk_e))
    y = jnp.zeros((T + 1, D), out.dtype).at[row_tok].add(out.reshape(n_rows, D))
    return y[:T]


def trunk_layer(x, lp, lam_init, s0, prev_row, attend):
    b, L, _ = x.shape
    h = rms_norm(x, lp['norm1_g']).astype(x.dtype)
    proj = h @ lp['w_in']
    pr, pa, pg = jnp.split(proj, [RWKV_PROJ, RWKV_PROJ + ATT_PROJ], axis=-1)
    o_r, s_fin, last_row = rwkv7_mix(pr, prev_row, s0, lp['shift_mu'], lp['w0'], lp['w_up'],
                                     lp['a0'], lp['a_up'], lp['g_up'], lp['k_k'], lp['k_a'],
                                     lp['r_k'], lp['lnx_g'], lp['lnx_b'])
    q, k, v = diff_attn_qkv(pa, lp['q_norm_g'], lp['k_norm_g'])
    f32 = jnp.float32
    lam = (jnp.exp(jnp.sum(lp['lam_q1'].astype(f32) * lp['lam_k1'].astype(f32)))
           - jnp.exp(jnp.sum(lp['lam_q2'].astype(f32) * lp['lam_k2'].astype(f32))) + lam_init)
    o_a = attend(q, k, v, lam)
    o_a = (rms_norm(o_a, lp['subln_g']) * (1.0 - lam_init)).reshape(b, L, ATT_DIM)
    gates = jax.nn.sigmoid(pg.astype(f32))
    gate_r, gate_a = gates[..., :D_MODEL], gates[..., D_MODEL:]
    merged = gate_r * (o_r @ lp['w_br_r']) + gate_a * (o_a @ lp['w_br_a'])
    x = x + (merged @ lp['w_out']).astype(x.dtype)
    h2 = rms_norm(x, lp['norm2_g']).astype(x.dtype)
    ff = moe_ffn(h2.reshape(b * L, D_MODEL), lp['router_w'], lp['router_b'],
                 lp['w_gu'], lp['b_gu'], lp['w_dn'], lp['b_dn'])
    x = x + ff.reshape(b, L, D_MODEL).astype(x.dtype)
    return x, k.astype(x.dtype), v.astype(x.dtype), s_fin.astype(x.dtype), last_row.astype(x.dtype)


def setup_inputs(seed: int = 0) -> dict:
    key = jax.random.key(seed)
    keys = iter(jax.random.split(key, 48))
    f32 = jnp.float32

    def nrm(shape, scale=1.0):
        return jax.random.normal(next(keys), shape, f32) * scale

    def unif(shape, lo, hi):
        return jax.random.uniform(next(keys), shape, f32, lo, hi)

    ones = lambda shape: 1.0 + nrm(shape, 0.02)
    return {
        'x_prompt': nrm((BATCH, SEQ, D_MODEL)),
        'x_sample': nrm((DEC_BATCH, DEC_SEQ, D_MODEL)),
        'cache_k': nrm((DEPTH, DEC_BATCH, PAST_LEN, ATT_HEADS, 2, QK_DIM)),
        'cache_v': nrm((DEPTH, DEC_BATCH, PAST_LEN, ATT_HEADS, V_DIM)),
        'state_wkv': nrm((DEPTH, DEC_BATCH, RWKV_HEADS, RWKV_HEAD, RWKV_HEAD), 0.5),
        'state_shift': nrm((DEPTH, DEC_BATCH, 1, RWKV_PROJ)),
        'meta': nrm((N_META, D_MODEL)),
        'norm1_g': ones((DEPTH, D_MODEL)),
        'w_in': nrm((DEPTH, D_MODEL, IN_PROJ), D_MODEL ** -0.5),
        'shift_mu': unif((DEPTH, RWKV_PROJ), 0.0, 1.0),
        'w0': unif((DEPTH, RWKV_DIM), -3.0, 1.0),
        'w_up': nrm((DEPTH, DECAY_LORA, RWKV_DIM), DECAY_LORA ** -0.5),
        'a0': nrm((DEPTH, RWKV_DIM), 0.5),
        'a_up': nrm((DEPTH, AAA_LORA, RWKV_DIM), AAA_LORA ** -0.5),
        'g_up': nrm((DEPTH, GATE_LORA, RWKV_DIM), GATE_LORA ** -0.5),
        'k_k': 0.85 + nrm((DEPTH, RWKV_DIM), 0.05),
        'k_a': 1.0 + nrm((DEPTH, RWKV_DIM), 0.05),
        'r_k': nrm((DEPTH, RWKV_HEADS, RWKV_HEAD), 0.1),
        'lnx_g': ones((DEPTH, RWKV_DIM)),
        'lnx_b': nrm((DEPTH, RWKV_DIM), 0.02),
        'q_norm_g': ones((DEPTH, QK_DIM)),
        'k_norm_g': ones((DEPTH, QK_DIM)),
        'lam_q1': nrm((DEPTH, QK_DIM), 0.1),
        'lam_k1': nrm((DEPTH, QK_DIM), 0.1),
        'lam_q2': nrm((DEPTH, QK_DIM), 0.1),
        'lam_k2': nrm((DEPTH, QK_DIM), 0.1),
        'subln_g': ones((DEPTH, V_DIM)),
        'w_br_r': nrm((DEPTH, RWKV_DIM, D_MODEL), RWKV_DIM ** -0.5),
        'w_br_a': nrm((DEPTH, ATT_DIM, D_MODEL), ATT_DIM ** -0.5),
        'w_out': nrm((DEPTH, D_MODEL, D_MODEL), D_MODEL ** -0.5),
        'norm2_g': ones((DEPTH, D_MODEL)),
        'router_w': nrm((DEPTH, D_MODEL, N_EXPERTS), D_MODEL ** -0.5),
        'router_b': nrm((DEPTH, N_EXPERTS), 0.01),
        'w_gu': nrm((DEPTH, N_EXPERTS, D_MODEL, 2 * D_FF), D_MODEL ** -0.5),
        'b_gu': nrm((DEPTH, N_EXPERTS, 2 * D_FF), 0.01),
        'w_dn': nrm((DEPTH, N_EXPERTS, D_FF, D_MODEL), D_FF ** -0.5),
        'b_dn': nrm((DEPTH, N_EXPERTS, D_MODEL), 0.01),
    }


def reference(x_prompt, x_sample, cache_k, cache_v, state_wkv, state_shift, meta,
              norm1_g, w_in, shift_mu, w0, w_up, a0, a_up, g_up, k_k, k_a, r_k, lnx_g, lnx_b,
              q_norm_g, k_norm_g, lam_q1, lam_k1, lam_q2, lam_k2, subln_g,
              w_br_r, w_br_a, w_out, norm2_g, router_w, router_b, w_gu, b_gu, w_dn, b_dn):
    slopes = jnp.asarray(alibi_slopes(ATT_HEADS))
    n_prompt = x_prompt.shape[0]
    meta_rows = jnp.broadcast_to(meta.astype(x_prompt.dtype)[None], (n_prompt, N_META, D_MODEL))
    xp = jnp.concatenate([meta_rows, x_prompt], axis=1)
    xs = x_sample
    attend_prompt = functools.partial(prompt_attention, slopes=slopes)
    st_p, st_s = [], []
    for l in range(DEPTH):
        lp = dict(norm1_g=norm1_g[l], w_in=w_in[l], shift_mu=shift_mu[l], w0=w0[l], w_up=w_up[l],
                  a0=a0[l], a_up=a_up[l], g_up=g_up[l], k_k=k_k[l], k_a=k_a[l], r_k=r_k[l],
                  lnx_g=lnx_g[l], lnx_b=lnx_b[l], q_norm_g=q_norm_g[l], k_norm_g=k_norm_g[l],
                  lam_q1=lam_q1[l], lam_k1=lam_k1[l], lam_q2=lam_q2[l], lam_k2=lam_k2[l],
                  subln_g=subln_g[l], w_br_r=w_br_r[l], w_br_a=w_br_a[l], w_out=w_out[l],
                  norm2_g=norm2_g[l], router_w=router_w[l], router_b=router_b[l],
                  w_gu=w_gu[l], b_gu=b_gu[l], w_dn=w_dn[l], b_dn=b_dn[l])
        li = lambda_init(l)
        s0 = jnp.zeros((n_prompt, RWKV_HEADS, RWKV_HEAD, RWKV_HEAD), jnp.float32)
        r0 = jnp.zeros((n_prompt, 1, RWKV_PROJ), x_prompt.dtype)
        xp, kp, vp, sp, rp = trunk_layer(xp, lp, li, s0, r0, attend_prompt)
        attend_sample = functools.partial(sample_attention, slopes=slopes,
                                          cache_k=cache_k[l], cache_v=cache_v[l])
        xs, ks, vs, ss, rs = trunk_layer(xs, lp, li, state_wkv[l], state_shift[l], attend_sample)
        st_p.append((kp, vp, sp, rp))
        st_s.append((ks, vs, ss, rs))
    y_prompt = xp[:, N_META:]
    y_sample = xs
    k_prompt = jnp.stack([t[0] for t in st_p])
    v_prompt = jnp.stack([t[1] for t in st_p])
    wkv_prompt = jnp.stack([t[2] for t in st_p])
    shift_prompt = jnp.stack([t[3] for t in st_p])
    k_sample = jnp.stack([t[0] for t in st_s])
    v_sample = jnp.stack([t[1] for t in st_s])
    wkv_sample = jnp.stack([t[2] for t in st_s])
    shift_sample = jnp.stack([t[3] for t in st_s])
    return (y_prompt, y_sample, k_prompt, v_prompt, wkv_prompt, shift_prompt,
            k_sample, v_sample, wkv_sample, shift_sample)
```

```python
import functools
import math

import jax
import jax.numpy as jnp
import numpy as np
from jax import lax
from jax.experimental import pallas as pl
from jax.experimental.pallas import tpu as pltpu

F32 = jnp.float32
BF16 = jnp.bfloat16

D_MODEL = 1024
CHUNK = 64
CHUNK_SHIFT = CHUNK.bit_length() - 1
assert 1 << CHUNK_SHIFT == CHUNK
N_META = 16
RWKV_HEAD = 64
RWKV_HEADS = D_MODEL // RWKV_HEAD
RWKV_DIM = RWKV_HEADS * RWKV_HEAD
DECAY_LORA = 64
AAA_LORA = 64
GATE_LORA = 128
RWKV_PROJ = 3 * RWKV_DIM + DECAY_LORA + AAA_LORA + GATE_LORA
LNX_EPS = 64e-5
ATT_HEADS = 8
QK_DIM = 64
V_DIM = 2 * QK_DIM
ATT_QK = ATT_HEADS * 2 * QK_DIM
ATT_DIM = ATT_HEADS * V_DIM
N_EXPERTS = 32
TOP_K = 4
SWIGLU_ALPHA = 1.702
SWIGLU_LIMIT = 7.0
RMS_EPS = 1e-6
NEG_INF = -1e30

LANES = 128
SUBLANES = 8
BF16_ROWS = 16
MXU_DIM = 256
VMEM_LIMIT = 56 * 1024 * 1024

GROUP_BLOCK = MXU_DIM
MOE_BLOCK = 256
ATT_TQ = 256
ATT_TK = 256
SCAN_TT = 48


def _cparams(*sem):
    return pltpu.CompilerParams(dimension_semantics=sem, vmem_limit_bytes=VMEM_LIMIT)


def _tile(n, target, mult):
    best = None
    for t in range(mult, min(n, target) + 1, mult):
        if n % t == 0:
            best = t
    return best if best is not None else n


def _group_ones():
    g = np.arange(GROUP_BLOCK) // RWKV_HEAD
    return jnp.asarray((g[:, None] == g[None, :]).astype(np.float32), BF16)


def _gsum(x, ones_ref):
    hi = x.astype(BF16)
    lo = (x - hi.astype(F32)).astype(BF16)
    ones = ones_ref[...]
    outs = []
    for c in range(x.shape[1] // GROUP_BLOCK):
        sl = slice(c * GROUP_BLOCK, (c + 1) * GROUP_BLOCK)
        outs.append(jnp.dot(hi[:, sl], ones, preferred_element_type=F32)
                    + jnp.dot(lo[:, sl], ones, preferred_element_type=F32))
    return jnp.concatenate(outs, axis=1)


def _norm_matmul_kernel(x_ref, g_ref, w_ref, o_ref):
    x = x_ref[...]
    h = x * lax.rsqrt(jnp.mean(x * x, axis=-1, keepdims=True) + RMS_EPS) * g_ref[...]
    o_ref[...] = jnp.dot(h.astype(BF16), w_ref[...], preferred_element_type=F32)


def _norm_matmul(x, g, w, tm):
    T, D = x.shape
    N = w.shape[1]
    return pl.pallas_call(
        _norm_matmul_kernel,
        grid=(T // tm,),
        in_specs=[pl.BlockSpec((tm, D), lambda i: (i, 0)),
                  pl.BlockSpec((1, D), lambda i: (0, 0)),
                  pl.BlockSpec((D, N), lambda i: (0, 0))],
        out_specs=pl.BlockSpec((tm, N), lambda i: (i, 0)),
        out_shape=jax.ShapeDtypeStruct((T, N), F32),
        compiler_params=_cparams("parallel"),
        name="norm_matmul",
    )(x, g.reshape(1, D), w)


def _rwkv_prep_kernel(pr_ref, prev_ref, mu_ref, w0_ref, wup_ref, a0_ref, aup_ref, gup_ref,
                      kk_ref, ka_ref, rk_ref, ones_ref,
                      kn_o, d_o, b_o, k_o, v_o, r_o, bonus_o, g_o):
    pr = pr_ref[...]
    row = lax.broadcasted_iota(jnp.int32, pr.shape, 0)
    prev = jnp.where(row == 0, prev_ref[...], pltpu.roll(pr, 1, axis=0))
    xs = pr + (prev - pr) * mu_ref[...]
    r = xs[:, :RWKV_DIM]
    k = xs[:, RWKV_DIM:2 * RWKV_DIM]
    v = xs[:, 2 * RWKV_DIM:3 * RWKV_DIM]
    lora_in = xs[:, 3 * RWKV_DIM:3 * RWKV_DIM + DECAY_LORA + AAA_LORA]
    gd = xs[:, 3 * RWKV_DIM + DECAY_LORA + AAA_LORA:]
    wl = jnp.dot(jnp.tanh(lora_in).astype(BF16), wup_ref[...], preferred_element_type=F32)
    al = jnp.dot(lora_in.astype(BF16), aup_ref[...], preferred_element_type=F32)
    z = -(w0_ref[...] + wl)
    w = -(jnp.maximum(z, 0.0) + jnp.log1p(jnp.exp(-jnp.abs(z)))) - 0.5
    decay = jnp.exp(-jnp.exp(w))
    a = jax.nn.sigmoid(a0_ref[...] + al)
    g = jnp.dot(jax.nn.sigmoid(gd).astype(BF16), gup_ref[...], preferred_element_type=F32)
    kk = k * kk_ref[...]
    nrm = jnp.sqrt(_gsum(kk * kk, ones_ref))
    kk = kk / jnp.maximum(nrm, 1e-12)
    k = k * (1.0 + (a - 1.0) * ka_ref[...])
    kn_o[...] = -kk
    d_o[...] = decay
    b_o[...] = kk * a
    k_o[...] = k
    v_o[...] = v
    r_o[...] = r
    bonus_o[...] = _gsum(r * k * rk_ref[...], ones_ref) * v
    g_o[...] = g


def _rwkv_prep(pr, prev_rows, lp, tl):
    T = pr.shape[0]
    nt = T // tl
    row = lambda a: a.reshape(1, -1).astype(F32)
    zpad = jnp.zeros((DECAY_LORA, RWKV_DIM), F32)
    wup = jnp.concatenate([lp['w_up'], zpad], axis=0).astype(BF16)
    aup = jnp.concatenate([zpad, lp['a_up']], axis=0).astype(BF16)
    vec = lambda n: pl.BlockSpec((1, n), lambda i: (0, 0))
    mat = lambda m, n: pl.BlockSpec((m, n), lambda i: (0, 0))
    tok = pl.BlockSpec((tl, RWKV_DIM), lambda i: (i, 0))
    return pl.pallas_call(
        _rwkv_prep_kernel,
        grid=(nt,),
        in_specs=[pl.BlockSpec((tl, RWKV_PROJ), lambda i: (i, 0)),
                  pl.BlockSpec((None, 1, RWKV_PROJ), lambda i: (i, 0, 0)),
                  vec(RWKV_PROJ), vec(RWKV_DIM), mat(2 * DECAY_LORA, RWKV_DIM), vec(RWKV_DIM),
                  mat(2 * AAA_LORA, RWKV_DIM), mat(GATE_LORA, RWKV_DIM),
                  vec(RWKV_DIM), vec(RWKV_DIM), vec(RWKV_DIM), mat(GROUP_BLOCK, GROUP_BLOCK)],
        out_specs=[tok] * 8,
        out_shape=[jax.ShapeDtypeStruct((T, RWKV_DIM), F32)] * 8,
        compiler_params=_cparams("parallel"),
        name="rwkv_prep",
    )(pr, prev_rows, row(lp['shift_mu']), row(lp['w0']), wup, row(lp['a0']), aup,
      lp['g_up'].astype(BF16), row(lp['k_k']), row(lp['k_a']), row(lp['r_k']), _group_ones())


def _rwkv_scan_kernel(kn_ref, d_ref, b_ref, k_ref, v_ref, r_ref, s0_ref, y_ref, sfin_ref, s_sc):
    tb = pl.program_id(1)

    @pl.when(tb == 0)
    def _():
        s_sc[...] = s0_ref[...]

    tt = kn_ref.shape[0]

    def tstep(t, carry):
        kn = kn_ref[t]
        d = d_ref[t]
        bb = b_ref[t]
        kk = k_ref[t]
        rr = r_ref[t]

        def vstep(vi, c):
            sv = s_sc[vi]
            sa = jnp.sum(sv * kn, axis=0, keepdims=True)
            vv = v_ref[t, pl.ds(vi, 1), :]
            sn = sv * d + sa * bb + vv * kk
            s_sc[vi] = sn
            y_ref[t, pl.ds(vi, 1), :] = jnp.sum(sn * rr, axis=0, keepdims=True)
            return c

        return lax.fori_loop(0, RWKV_HEAD, vstep, carry, unroll=4)

    lax.fori_loop(0, tt, tstep, 0)

    @pl.when(tb == pl.num_programs(1) - 1)
    def _():
        sfin_ref[...] = s_sc[...]


def _rwkv_scan(ops, s0, tt):
    L, n, nc = ops[0].shape
    tspec = pl.BlockSpec((tt, n, LANES), lambda c, t: (t, 0, c))
    sspec = pl.BlockSpec((n, n, LANES), lambda c, t: (0, 0, c))
    return pl.pallas_call(
        _rwkv_scan_kernel,
        grid=(nc // LANES, L // tt),
        in_specs=[tspec] * 6 + [sspec],
        out_specs=[tspec, sspec],
        out_shape=[jax.ShapeDtypeStruct((L, n, nc), F32), jax.ShapeDtypeStruct((n, n, nc), F32)],
        scratch_shapes=[pltpu.VMEM((n, n, LANES), F32)],
        compiler_params=_cparams("parallel", "arbitrary"),
        name="rwkv_scan",
    )(*ops, s0)


def _rwkv_post_kernel(y_ref, bonus_ref, g_ref, lg_ref, lb_ref, ones_ref, o_ref):
    y = y_ref[...]
    inv_n = 1.0 / RWKV_HEAD
    mean = _gsum(y, ones_ref) * inv_n
    yc = y - mean
    var = _gsum(yc * yc, ones_ref) * inv_n
    yn = yc * lax.rsqrt(var + LNX_EPS) * lg_ref[...] + lb_ref[...]
    o_ref[...] = ((yn + bonus_ref[...]) * g_ref[...]).astype(BF16)


def _rwkv_post(y, bonus, g, lp, tm):
    T = y.shape[0]
    tok = pl.BlockSpec((tm, RWKV_DIM), lambda i: (i, 0))
    vec = pl.BlockSpec((1, RWKV_DIM), lambda i: (0, 0))
    return pl.pallas_call(
        _rwkv_post_kernel,
        grid=(T // tm,),
        in_specs=[tok, tok, tok, vec, vec, pl.BlockSpec((GROUP_BLOCK, GROUP_BLOCK), lambda i: (0, 0))],
        out_specs=tok,
        out_shape=jax.ShapeDtypeStruct((T, RWKV_DIM), BF16),
        compiler_params=_cparams("parallel"),
        name="rwkv_post",
    )(y, bonus, g, lp['lnx_g'].reshape(1, -1), lp['lnx_b'].reshape(1, -1), _group_ones())


def _qk_norm_kernel(qk_ref, v_ref, qg_ref, kg_ref, ones_ref, kout_ref, qb_ref, kb_ref, vb_ref, *, length):
    tl = qk_ref.shape[0]
    valid = (pl.program_id(1) * tl + lax.broadcasted_iota(jnp.int32, (tl, 1), 0)) < length
    q = qk_ref[:, :ATT_QK]
    k = qk_ref[:, ATT_QK:]
    inv_n = 1.0 / QK_DIM
    qn = q * lax.rsqrt(_gsum(q * q, ones_ref) * inv_n + RMS_EPS) * qg_ref[...]
    kn = k * lax.rsqrt(_gsum(k * k, ones_ref) * inv_n + RMS_EPS) * kg_ref[...]
    kout_ref[...] = kn
    qb_ref[...] = jnp.where(valid, qn * (QK_DIM ** -0.5), 0.0).astype(BF16)
    kb_ref[...] = jnp.where(valid, kn, 0.0).astype(BF16)
    vb_ref[...] = jnp.where(valid, v_ref[...], 0.0).astype(BF16)


def _qk_norm(pqk, pv, lp, lpad, tl):
    B, L, _ = pqk.shape
    tile_g = lambda g: jnp.tile(g.astype(F32), ATT_QK // QK_DIM).reshape(1, ATT_QK)
    blk = lambda n: pl.BlockSpec((None, tl, n), lambda b, i: (b, i, 0))
    vec = pl.BlockSpec((1, ATT_QK), lambda b, i: (0, 0))
    bshape = jax.ShapeDtypeStruct((B, lpad, ATT_QK), BF16)
    return pl.pallas_call(
        functools.partial(_qk_norm_kernel, length=L),
        grid=(B, lpad // tl),
        in_specs=[blk(2 * ATT_QK), blk(ATT_DIM), vec, vec,
                  pl.BlockSpec((GROUP_BLOCK, GROUP_BLOCK), lambda b, i: (0, 0))],
        out_specs=[blk(ATT_QK)] * 4,
        out_shape=[jax.ShapeDtypeStruct((B, L, ATT_QK), F32), bshape, bshape, bshape],
        compiler_params=_cparams("parallel", "parallel"),
        name="qk_norm",
    )(pqk, pv, tile_g(lp['q_norm_g']), tile_g(lp['k_norm_g']), _group_ones())


def _attn_kernel(slope_ref, lam_ref, q_ref, k_ref, v_ref, g_ref, o_ref,
                 m1_sc, l1_sc, a1_sc, m2_sc, l2_sc, a2_sc,
                 *, chunked, q_pos0, n_keys, out_scale):
    tq = q_ref.shape[0]
    tk = ATT_TK
    h = pl.program_id(1)
    i = pl.program_id(2)
    q = q_ref[...]
    lane = lax.broadcasted_iota(jnp.int32, q.shape, 1)
    zero = jnp.zeros_like(q)
    q1 = jnp.where(lane < QK_DIM, q, zero)
    q2 = jnp.where(lane >= QK_DIM, q, zero)
    q_first = q_pos0 + i * tq
    qpos = q_first + lax.broadcasted_iota(jnp.int32, (tq, 1), 0)
    if chunked:
        chunk_end = lambda p: N_META + CHUNK + (((p - N_META) >> CHUNK_SHIFT) << CHUNK_SHIFT)
        vis = jnp.where(qpos < N_META, N_META, chunk_end(qpos))
        vis_last = chunk_end(q_first + tq - 1)
        n_kt = (jnp.minimum(vis_last, n_keys) + tk - 1) // tk
    else:
        vis = jnp.full((tq, 1), n_keys, jnp.int32)
        n_kt = (n_keys + tk - 1) // tk
    slope = slope_ref[h]

    for m_sc, l_sc, a_sc in ((m1_sc, l1_sc, a1_sc), (m2_sc, l2_sc, a2_sc)):
        m_sc[...] = jnp.full(m_sc.shape, -jnp.inf, F32)
        l_sc[...] = jnp.zeros(l_sc.shape, F32)
        a_sc[...] = jnp.zeros(a_sc.shape, F32)

    def body(j, carry):
        k0 = pl.multiple_of(j * tk, tk)
        kj = k_ref[pl.ds(k0, tk), :]
        vj = v_ref[pl.ds(k0, tk), :]
        kpos = k0 + lax.broadcasted_iota(jnp.int32, (1, tk), 1)
        bias = slope * jnp.abs(qpos - kpos).astype(F32)
        allowed = kpos < vis
        nt = (((1,), (1,)), ((), ()))

        def one_map(qm, m_sc, l_sc, a_sc):
            s = lax.dot_general(qm, kj, nt, preferred_element_type=F32) - bias
            s = jnp.where(allowed, s, NEG_INF)
            m_old = m_sc[...]
            m_new = jnp.maximum(m_old, jnp.max(s, axis=-1, keepdims=True))
            alpha = jnp.exp(m_old - m_new)
            p = jnp.exp(s - m_new)
            l_sc[...] = alpha * l_sc[...] + jnp.sum(p, axis=-1, keepdims=True)
            a_sc[...] = alpha * a_sc[...] + jnp.dot(p.astype(BF16), vj, preferred_element_type=F32)
            m_sc[...] = m_new

        one_map(q1, m1_sc, l1_sc, a1_sc)
        one_map(q2, m2_sc, l2_sc, a2_sc)
        return carry

    lax.fori_loop(0, n_kt, body, 0)

    o = a1_sc[...] / l1_sc[...] - lam_ref[0] * (a2_sc[...] / l2_sc[...])
    o = o * lax.rsqrt(jnp.mean(o * o, axis=-1, keepdims=True) + RMS_EPS) * g_ref[...] * out_scale
    o_ref[...] = o.astype(BF16)


def _attention(qb, kb, vb, lam, subln_g, lam_init, *, chunked, q_pos0, n_keys, tq):
    B, lq, _ = qb.shape
    lk = kb.shape[1]
    slopes = jnp.asarray([2.0 ** (-8.0 * (i + 1) / ATT_HEADS) for i in range(ATT_HEADS)], F32)
    smem = pl.BlockSpec(memory_space=pltpu.SMEM)
    qspec = pl.BlockSpec((None, tq, V_DIM), lambda b, h, i: (b, i, h))
    kspec = pl.BlockSpec((None, lk, V_DIM), lambda b, h, i: (b, 0, h))
    stat = pltpu.VMEM((tq, 1), F32)
    acc = pltpu.VMEM((tq, V_DIM), F32)
    return pl.pallas_call(
        functools.partial(_attn_kernel, chunked=chunked, q_pos0=q_pos0, n_keys=n_keys,
                          out_scale=1.0 - lam_init),
        grid=(B, ATT_HEADS, lq // tq),
        in_specs=[smem, smem, qspec, kspec, kspec, pl.BlockSpec((1, V_DIM), lambda b, h, i: (0, 0))],
        out_specs=qspec,
        out_shape=jax.ShapeDtypeStruct((B, lq, ATT_DIM), BF16),
        scratch_shapes=[stat, stat, acc, stat, stat, acc],
        compiler_params=_cparams("parallel", "parallel", "arbitrary"),
        name="diff_attention",
    )(slopes, lam.reshape(1).astype(F32), qb, kb, vb, subln_g.reshape(1, V_DIM).astype(F32))


def _merge_kernel(or_ref, oa_ref, pg_ref, x_ref, wr_ref, wa_ref, wo_ref, g2_ref, rw_ref, rb_ref,
                  x1_ref, h2_ref, lg_ref):
    gates = jax.nn.sigmoid(pg_ref[...])
    br = jnp.dot(or_ref[...], wr_ref[...], preferred_element_type=F32)
    ba = jnp.dot(oa_ref[...], wa_ref[...], preferred_element_type=F32)
    merged = gates[:, :D_MODEL] * br + gates[:, D_MODEL:] * ba
    x1 = x_ref[...] + jnp.dot(merged.astype(BF16), wo_ref[...], preferred_element_type=F32)
    h2 = x1 * lax.rsqrt(jnp.mean(x1 * x1, axis=-1, keepdims=True) + RMS_EPS) * g2_ref[...]
    x1_ref[...] = x1
    h2_ref[...] = h2.astype(BF16)
    lg_ref[...] = jnp.dot(h2, rw_ref[...], preferred_element_type=F32,
                          precision=lax.Precision.HIGHEST) + rb_ref[...]


def _merge(o_r, o_a, pg, x, lp, tm):
    T = x.shape[0]
    rw = jnp.zeros((D_MODEL, LANES), F32).at[:, :N_EXPERTS].set(lp['router_w'].astype(F32))
    rb = jnp.zeros((1, LANES), F32).at[0, :N_EXPERTS].set(lp['router_b'].astype(F32))
    tok = lambda n: pl.BlockSpec((tm, n), lambda i: (i, 0))
    mat = lambda m, n: pl.BlockSpec((m, n), lambda i: (0, 0))
    return pl.pallas_call(
        _merge_kernel,
        grid=(T // tm,),
        in_specs=[tok(RWKV_DIM), tok(ATT_DIM), tok(2 * D_MODEL), tok(D_MODEL),
                  mat(RWKV_DIM, D_MODEL), mat(ATT_DIM, D_MODEL), mat(D_MODEL, D_MODEL),
                  mat(1, D_MODEL), mat(D_MODEL, LANES), mat(1, LANES)],
        out_specs=[tok(D_MODEL), tok(D_MODEL), tok(LANES)],
        out_shape=[jax.ShapeDtypeStruct((T, D_MODEL), F32), jax.ShapeDtypeStruct((T, D_MODEL), BF16),
                   jax.ShapeDtypeStruct((T, LANES), F32)],
        compiler_params=_cparams("parallel"),
        name="merge_router",
    )(o_r, o_a, pg, x, lp['w_br_r'].astype(BF16), lp['w_br_a'].astype(BF16), lp['w_out'].astype(BF16),
      lp['norm2_g'].reshape(1, -1), rw, rb)


def _moe_kernel(be_ref, nact_ref, x_ref, gate_ref, wgu_ref, bgu_ref, wdn_ref, bdn_ref, o_ref):
    i = pl.program_id(0)

    @pl.when(i < nact_ref[0])
    def _():
        gu = jnp.dot(x_ref[...], wgu_ref[...], preferred_element_type=F32) + bgu_ref[...]
        d_ff = gu.shape[1] // 2
        x_glu = jnp.minimum(gu[:, :d_ff], SWIGLU_LIMIT)
        x_lin = jnp.clip(gu[:, d_ff:], -SWIGLU_LIMIT, SWIGLU_LIMIT)
        act = x_glu * jax.nn.sigmoid(SWIGLU_ALPHA * x_glu) * (x_lin + 1.0)
        out = jnp.dot(act.astype(BF16), wdn_ref[...], preferred_element_type=F32) + bdn_ref[...]
        o_ref[...] = out * gate_ref[...]

    @pl.when(i >= nact_ref[0])
    def _():
        o_ref[...] = jnp.zeros_like(o_ref)


def _moe_ffn_blocks(xs, row_gate, blk_e, n_act, w_gu, b_gu, w_dn, b_dn):
    n_rows, D = xs.shape
    n_blk = n_rows // MOE_BLOCK
    d_ff2 = w_gu.shape[2]
    grid_spec = pltpu.PrefetchScalarGridSpec(
        num_scalar_prefetch=2,
        grid=(n_blk,),
        in_specs=[pl.BlockSpec((MOE_BLOCK, D), lambda i, be, na: (i, 0)),
                  pl.BlockSpec((MOE_BLOCK, 1), lambda i, be, na: (i, 0)),
                  pl.BlockSpec((None, D, d_ff2), lambda i, be, na: (be[i], 0, 0)),
                  pl.BlockSpec((None, 1, d_ff2), lambda i, be, na: (be[i], 0, 0)),
                  pl.BlockSpec((None, d_ff2 // 2, D), lambda i, be, na: (be[i], 0, 0)),
                  pl.BlockSpec((None, 1, D), lambda i, be, na: (be[i], 0, 0))],
        out_specs=pl.BlockSpec((MOE_BLOCK, D), lambda i, be, na: (i, 0)),
    )
    return pl.pallas_call(
        _moe_kernel,
        grid_spec=grid_spec,
        out_shape=jax.ShapeDtypeStruct((n_rows, D), F32),
        compiler_params=_cparams("arbitrary"),
        name="moe_ffn",
    )(blk_e, n_act, xs, row_gate.reshape(n_rows, 1), w_gu, b_gu.reshape(N_EXPERTS, 1, d_ff2),
      w_dn, b_dn.reshape(N_EXPERTS, 1, D))


def _moe(h2, logits, x1, moe_w):
    T, D = h2.shape
    top_v, top_e = lax.top_k(logits, TOP_K)
    gate = jax.nn.softmax(top_v, axis=-1)
    rows = T * TOP_K
    flat_e = top_e.reshape(-1).astype(jnp.int32)
    order = jnp.argsort(flat_e).astype(jnp.int32)
    e_sorted = flat_e[order]
    counts = jnp.bincount(flat_e, length=N_EXPERTS).astype(jnp.int32)
    padded = (counts + MOE_BLOCK - 1) // MOE_BLOCK * MOE_BLOCK
    pad_end = jnp.cumsum(padded)
    pad_start = pad_end - padded
    start = jnp.cumsum(counts) - counts
    dest = pad_start[e_sorted] + jnp.arange(rows, dtype=jnp.int32) - start[e_sorted]
    n_blk = -(-(rows + N_EXPERTS * (MOE_BLOCK - 1)) // MOE_BLOCK)
    n_rows = n_blk * MOE_BLOCK
    row_tok = jnp.full((n_rows,), T, jnp.int32).at[dest].set(order // TOP_K)
    row_gate = jnp.zeros((n_rows,), F32).at[dest].set(gate.reshape(-1)[order])
    blk_start = jnp.arange(n_blk, dtype=jnp.int32) * MOE_BLOCK
    blk_e = jnp.minimum(jnp.searchsorted(pad_end, blk_start, side='right'), N_EXPERTS - 1).astype(jnp.int32)
    n_act = (pad_end[-1:] // MOE_BLOCK).astype(jnp.int32)
    pos = jnp.zeros((rows,), jnp.int32).at[order].set(dest).reshape(T, TOP_K)

    h_pad = jnp.concatenate([h2, jnp.zeros((1, D), h2.dtype)], axis=0)
    xs = h_pad[row_tok]
    out = _moe_ffn_blocks(xs, row_gate, blk_e, n_act, *moe_w)
    return x1 + jnp.sum(out[pos], axis=1)


def _to_chains(a, B, L):
    return a.reshape(B, L, RWKV_HEADS, RWKV_HEAD).transpose(1, 3, 0, 2).reshape(L, RWKV_HEAD, B * RWKV_HEADS)


def _trunk_layer(x, lp, w_bf, moe_w, lam, lam_init, s0, prev_row, cache=None):
    B, L, D = x.shape
    T = B * L
    x2 = x.reshape(T, D)
    tl = _tile(L, 320, BF16_ROWS)
    tm = _tile(T, 1024, BF16_ROWS) if cache is not None else _tile(L, 1024, BF16_ROWS)
    tm_small = _tile(tm, 320, BF16_ROWS)
    w_r, w_qk, w_v, w_g = w_bf

    pr = _norm_matmul(x2, lp['norm1_g'], w_r, tm_small)
    pqk = _norm_matmul(x2, lp['norm1_g'], w_qk, tm)
    pv = _norm_matmul(x2, lp['norm1_g'], w_v, tm)
    pg = _norm_matmul(x2, lp['norm1_g'], w_g, tm)

    pr3 = pr.reshape(B, L, RWKV_PROJ)
    tile_last = pr3.reshape(B, L // tl, tl, RWKV_PROJ)[:, :-1, -1]
    prev_rows = jnp.concatenate([prev_row.astype(F32), tile_last], axis=1).reshape(T // tl, 1, RWKV_PROJ)
    kn, dec, bb, kmod, vv, rr, bonus, g = _rwkv_prep(pr, prev_rows, lp, tl)
    ops = [_to_chains(a, B, L) for a in (kn, dec, bb, kmod, vv, rr)]
    nc = B * RWKV_HEADS
    s0c = s0.astype(F32).transpose(2, 3, 0, 1).reshape(RWKV_HEAD, RWKV_HEAD, nc)
    y_c, s_fin_c = _rwkv_scan(ops, s0c, _tile(L, SCAN_TT, 1))
    y = y_c.reshape(L, RWKV_HEAD, B, RWKV_HEADS).transpose(2, 0, 3, 1).reshape(T, RWKV_DIM)
    s_fin = s_fin_c.reshape(RWKV_HEAD, RWKV_HEAD, B, RWKV_HEADS).transpose(2, 3, 0, 1)
    o_r = _rwkv_post(y, bonus, g, lp, tl)

    if cache is None:
        tq = ATT_TQ
        lpad = -(-L // max(tq, ATT_TK)) * max(tq, ATT_TK)
        k_rows, qb, kb, vb = _qk_norm(pqk.reshape(B, L, 2 * ATT_QK), pv.reshape(B, L, ATT_DIM), lp, lpad,
                                      _tile(lpad, 512, BF16_ROWS))
        o_a = _attention(qb, kb, vb, lam, lp['subln_g'], lam_init, chunked=True, q_pos0=0, n_keys=L, tq=tq)
    else:
        cache_k, cache_v = cache
        past = cache_k.shape[1]
        k_rows, qb, kb, vb = _qk_norm(pqk.reshape(B, L, 2 * ATT_QK), pv.reshape(B, L, ATT_DIM), lp, L, L)
        n_keys = past + L
        lk_pad = -(-n_keys // ATT_TK) * ATT_TK
        zk = jnp.zeros((B, lk_pad - n_keys, ATT_QK), BF16)
        k_all = jnp.concatenate([cache_k.reshape(B, past, ATT_QK).astype(BF16), kb, zk], axis=1)
        v_all = jnp.concatenate([cache_v.reshape(B, past, ATT_DIM).astype(BF16), vb, zk], axis=1)
        o_a = _attention(qb, k_all, v_all, lam, lp['subln_g'], lam_init, chunked=False, q_pos0=past,
                         n_keys=n_keys, tq=L)
    o_a = o_a[:, :L].reshape(T, ATT_DIM)

    x1, h2, logits = _merge(o_r, o_a, pg, x2, lp, tm_small)
    x_out = _moe(h2, logits[:, :N_EXPERTS], x1, moe_w)
    return (x_out.reshape(B, L, D), k_rows.reshape(B, L, ATT_HEADS, 2, QK_DIM),
            pv.reshape(B, L, ATT_HEADS, V_DIM), s_fin, pr3[:, -1:])


def kernel(x_prompt, x_sample, cache_k, cache_v, state_wkv, state_shift, meta, norm1_g, w_in, shift_mu, w0, w_up, a0, a_up, g_up, k_k, k_a, r_k, lnx_g, lnx_b, q_norm_g, k_norm_g, lam_q1, lam_k1, lam_q2, lam_k2, subln_g, w_br_r, w_br_a, w_out, norm2_g, router_w, router_b, w_gu, b_gu, w_dn, b_dn):
    depth = w_in.shape[0]
    n_prompt = x_prompt.shape[0]
    meta_rows = jnp.broadcast_to(meta.astype(x_prompt.dtype)[None], (n_prompt, N_META, D_MODEL))
    xp = jnp.concatenate([meta_rows, x_prompt], axis=1)
    xs = x_sample
    st_p, st_s = [], []
    for l in range(depth):
        lp = dict(norm1_g=norm1_g[l], shift_mu=shift_mu[l], w0=w0[l], w_up=w_up[l],
                  a0=a0[l], a_up=a_up[l], g_up=g_up[l], k_k=k_k[l], k_a=k_a[l], r_k=r_k[l].reshape(-1),
                  lnx_g=lnx_g[l], lnx_b=lnx_b[l], q_norm_g=q_norm_g[l], k_norm_g=k_norm_g[l],
                  subln_g=subln_g[l], w_br_r=w_br_r[l], w_br_a=w_br_a[l], w_out=w_out[l],
                  norm2_g=norm2_g[l], router_w=router_w[l], router_b=router_b[l])
        wl = w_in[l].astype(BF16)
        c0, c1, c2 = RWKV_PROJ, RWKV_PROJ + 2 * ATT_QK, RWKV_PROJ + 2 * ATT_QK + ATT_DIM
        w_bf = (wl[:, :c0], wl[:, c0:c1], wl[:, c1:c2], wl[:, c2:])
        moe_w = (w_gu[l].astype(BF16), b_gu[l].astype(F32), w_dn[l].astype(BF16), b_dn[l].astype(F32))
        lam_init = 0.8 - 0.6 * math.exp(-0.3 * l)
        lam = (jnp.exp(jnp.sum(lam_q1[l].astype(F32) * lam_k1[l].astype(F32)))
               - jnp.exp(jnp.sum(lam_q2[l].astype(F32) * lam_k2[l].astype(F32))) + lam_init)
        s0 = jnp.zeros((n_prompt, RWKV_HEADS, RWKV_HEAD, RWKV_HEAD), F32)
        r0 = jnp.zeros((n_prompt, 1, RWKV_PROJ), F32)
        xp, kp, vp, sp, rp = _trunk_layer(xp, lp, w_bf, moe_w, lam, lam_init, s0, r0)
        xs, ks, vs, ss, rs = _trunk_layer(xs, lp, w_bf, moe_w, lam, lam_init, state_wkv[l], state_shift[l],
                                          cache=(cache_k[l], cache_v[l]))
        st_p.append((kp, vp, sp, rp))
        st_s.append((ks, vs, ss, rs))
    stack = lambda st, j: jnp.stack([t[j] for t in st])
    return (xp[:, N_META:], xs,
            stack(st_p, 0), stack(st_p, 1), stack(st_p, 2), stack(st_p, 3),
            stack(st_s, 0), stack(st_s, 1), stack(st_s, 2), stack(st_s, 3))
```

```python
import functools
import math

import jax
import jax.numpy as jnp
import numpy as np
from jax import lax
from jax.experimental import pallas as pl
from jax.experimental.pallas import tpu as pltpu

F32 = jnp.float32
BF16 = jnp.bfloat16

D_MODEL = 1024
CHUNK = 64
CHUNK_SHIFT = CHUNK.bit_length() - 1
assert 1 << CHUNK_SHIFT == CHUNK
N_META = 16
RWKV_HEAD = 64
RWKV_HEADS = D_MODEL // RWKV_HEAD
RWKV_DIM = RWKV_HEADS * RWKV_HEAD
DECAY_LORA = 64
AAA_LORA = 64
GATE_LORA = 128
RWKV_PROJ = 3 * RWKV_DIM + DECAY_LORA + AAA_LORA + GATE_LORA
LNX_EPS = 64e-5
ATT_HEADS = 8
QK_DIM = 64
V_DIM = 2 * QK_DIM
ATT_QK = ATT_HEADS * 2 * QK_DIM
ATT_DIM = ATT_HEADS * V_DIM
N_EXPERTS = 32
TOP_K = 4
SWIGLU_ALPHA = 1.702
SWIGLU_LIMIT = 7.0
RMS_EPS = 1e-6
NEG_INF = -1e30
LOG2E = math.log2(math.e)
Q_SCALE = QK_DIM ** -0.5 * LOG2E

LANES = 128
SUBLANES = 8
BF16_ROWS = 16
MXU_DIM = 256
VMEM_LIMIT = 56 * 1024 * 1024

GROUP_BLOCK = MXU_DIM
MOE_BLOCK = 256
ATT_TQ = 256
ATT_TK = 512
SCAN_TT = 48


def _cparams(*sem):
    return pltpu.CompilerParams(dimension_semantics=sem, vmem_limit_bytes=VMEM_LIMIT)


def _tile(n, target, mult):
    best = None
    for t in range(mult, min(n, target) + 1, mult):
        if n % t == 0:
            best = t
    return best if best is not None else n


def _group_ones():
    g = np.arange(GROUP_BLOCK) // RWKV_HEAD
    return jnp.asarray((g[:, None] == g[None, :]).astype(np.float32), BF16)


def _gsum(x, ones_ref):
    hi = x.astype(BF16)
    lo = (x - hi.astype(F32)).astype(BF16)
    ones = ones_ref[...]
    outs = []
    for c in range(x.shape[1] // GROUP_BLOCK):
        sl = slice(c * GROUP_BLOCK, (c + 1) * GROUP_BLOCK)
        outs.append(jnp.dot(hi[:, sl], ones, preferred_element_type=F32)
                    + jnp.dot(lo[:, sl], ones, preferred_element_type=F32))
    return jnp.concatenate(outs, axis=1)


def _norm_matmul_kernel(x_ref, g_ref, w_ref, o_ref):
    x = x_ref[...]
    h = x * lax.rsqrt(jnp.mean(x * x, axis=-1, keepdims=True) + RMS_EPS) * g_ref[...]
    o_ref[...] = jnp.dot(h.astype(BF16), w_ref[...], preferred_element_type=F32)


def _norm_matmul(x, g, w, tm):
    T, D = x.shape
    N = w.shape[1]
    return pl.pallas_call(
        _norm_matmul_kernel,
        grid=(T // tm,),
        in_specs=[pl.BlockSpec((tm, D), lambda i: (i, 0)),
                  pl.BlockSpec((1, D), lambda i: (0, 0)),
                  pl.BlockSpec((D, N), lambda i: (0, 0))],
        out_specs=pl.BlockSpec((tm, N), lambda i: (i, 0)),
        out_shape=jax.ShapeDtypeStruct((T, N), F32),
        compiler_params=_cparams("parallel"),
        name="norm_matmul",
    )(x, g.reshape(1, D), w)


def _rwkv_prep_kernel(pr_ref, prev_ref, mu_ref, w0_ref, wup_ref, a0_ref, aup_ref, gup_ref,
                      kk_ref, ka_ref, rk_ref, ones_ref,
                      kn_o, d_o, b_o, k_o, v_o, r_o, bonus_o, g_o):
    pr = pr_ref[...]
    row = lax.broadcasted_iota(jnp.int32, pr.shape, 0)
    prev = jnp.where(row == 0, prev_ref[...], pltpu.roll(pr, 1, axis=0))
    xs = pr + (prev - pr) * mu_ref[...]
    r = xs[:, :RWKV_DIM]
    k = xs[:, RWKV_DIM:2 * RWKV_DIM]
    v = xs[:, 2 * RWKV_DIM:3 * RWKV_DIM]
    lora_in = xs[:, 3 * RWKV_DIM:3 * RWKV_DIM + DECAY_LORA + AAA_LORA]
    gd = xs[:, 3 * RWKV_DIM + DECAY_LORA + AAA_LORA:]
    wl = jnp.dot(jnp.tanh(lora_in).astype(BF16), wup_ref[...], preferred_element_type=F32)
    al = jnp.dot(lora_in.astype(BF16), aup_ref[...], preferred_element_type=F32)
    z = -(w0_ref[...] + wl)
    w = -(jnp.maximum(z, 0.0) + jnp.log1p(jnp.exp(-jnp.abs(z)))) - 0.5
    decay = jnp.exp(-jnp.exp(w))
    a = jax.nn.sigmoid(a0_ref[...] + al)
    g = jnp.dot(jax.nn.sigmoid(gd).astype(BF16), gup_ref[...], preferred_element_type=F32)
    kk = k * kk_ref[...]
    nrm = jnp.sqrt(_gsum(kk * kk, ones_ref))
    kk = kk / jnp.maximum(nrm, 1e-12)
    k = k * (1.0 + (a - 1.0) * ka_ref[...])
    kn_o[...] = -kk
    d_o[...] = decay
    b_o[...] = kk * a
    k_o[...] = k
    v_o[...] = v
    r_o[...] = r
    bonus_o[...] = _gsum(r * k * rk_ref[...], ones_ref) * v
    g_o[...] = g


def _rwkv_prep(pr, prev_rows, lp, B, L, tl):
    nt = L // tl
    row = lambda a: a.reshape(1, -1).astype(F32)
    zpad = jnp.zeros((DECAY_LORA, RWKV_DIM), F32)
    wup = jnp.concatenate([lp['w_up'], zpad], axis=0).astype(BF16)
    aup = jnp.concatenate([zpad, lp['a_up']], axis=0).astype(BF16)
    vec = lambda n: pl.BlockSpec((1, n), lambda b, i: (0, 0))
    mat = lambda m, n: pl.BlockSpec((m, n), lambda b, i: (0, 0))
    tok = pl.BlockSpec((tl, RWKV_DIM), lambda b, i: (b * nt + i, 0))
    time_major = pl.BlockSpec((tl, RWKV_DIM), lambda b, i: (i, b))
    return pl.pallas_call(
        _rwkv_prep_kernel,
        grid=(B, nt),
        in_specs=[pl.BlockSpec((tl, RWKV_PROJ), lambda b, i: (b * nt + i, 0)),
                  pl.BlockSpec((None, 1, RWKV_PROJ), lambda b, i: (b * nt + i, 0, 0)),
                  vec(RWKV_PROJ), vec(RWKV_DIM), mat(2 * DECAY_LORA, RWKV_DIM), vec(RWKV_DIM),
                  mat(2 * AAA_LORA, RWKV_DIM), mat(GATE_LORA, RWKV_DIM),
                  vec(RWKV_DIM), vec(RWKV_DIM), vec(RWKV_DIM), mat(GROUP_BLOCK, GROUP_BLOCK)],
        out_specs=[time_major] * 6 + [tok] * 2,
        out_shape=[jax.ShapeDtypeStruct((L, B * RWKV_DIM), F32)] * 6
                  + [jax.ShapeDtypeStruct((B * L, RWKV_DIM), F32)] * 2,
        compiler_params=_cparams("parallel", "parallel"),
        name="rwkv_prep",
    )(pr, prev_rows, row(lp['shift_mu']), row(lp['w0']), wup, row(lp['a0']), aup,
      lp['g_up'].astype(BF16), row(lp['k_k']), row(lp['k_a']), row(lp['r_k']), _group_ones())


def _rwkv_scan_kernel(kn_ref, d_ref, b_ref, k_ref, v_ref, r_ref, s0_ref, y_ref, sfin_ref, s_sc):
    tb = pl.program_id(1)

    @pl.when(tb == 0)
    def _():
        s_sc[...] = s0_ref[...]

    tt = kn_ref.shape[0]

    def tstep(t, carry):
        kn = kn_ref[t]
        d = d_ref[t]
        bb = b_ref[t]
        kk = k_ref[t]
        rr = r_ref[t]

        def vstep(vi, c):
            sv = s_sc[vi]
            sa = jnp.sum(sv * kn, axis=0, keepdims=True)
            vv = v_ref[t, pl.ds(vi, 1), :]
            sn = sv * d + sa * bb + vv * kk
            s_sc[vi] = sn
            y_ref[t, pl.ds(vi, 1), :] = jnp.sum(sn * rr, axis=0, keepdims=True)
            return c

        return lax.fori_loop(0, RWKV_HEAD, vstep, carry, unroll=4)

    lax.fori_loop(0, tt, tstep, 0)

    @pl.when(tb == pl.num_programs(1) - 1)
    def _():
        sfin_ref[...] = s_sc[...]


def _rwkv_scan(ops, s0, tt):
    L, n, nc = ops[0].shape
    tspec = pl.BlockSpec((tt, n, LANES), lambda c, t: (t, 0, c))
    sspec = pl.BlockSpec((n, n, LANES), lambda c, t: (0, 0, c))
    return pl.pallas_call(
        _rwkv_scan_kernel,
        grid=(nc // LANES, L // tt),
        in_specs=[tspec] * 6 + [sspec],
        out_specs=[tspec, sspec],
        out_shape=[jax.ShapeDtypeStruct((L, n, nc), F32), jax.ShapeDtypeStruct((n, n, nc), F32)],
        scratch_shapes=[pltpu.VMEM((n, n, LANES), F32)],
        compiler_params=_cparams("parallel", "arbitrary"),
        name="rwkv_scan",
    )(*ops, s0)


def _rwkv_post_kernel(y_ref, bonus_ref, g_ref, lg_ref, lb_ref, ones_ref, o_ref):
    y = y_ref[...]
    inv_n = 1.0 / RWKV_HEAD
    mean = _gsum(y, ones_ref) * inv_n
    yc = y - mean
    var = _gsum(yc * yc, ones_ref) * inv_n
    yn = yc * lax.rsqrt(var + LNX_EPS) * lg_ref[...] + lb_ref[...]
    o_ref[...] = ((yn + bonus_ref[...]) * g_ref[...]).astype(BF16)


def _rwkv_post(y, bonus, g, lp, B, L, tl):
    nt = L // tl
    tok = pl.BlockSpec((tl, RWKV_DIM), lambda b, i: (b * nt + i, 0))
    vec = pl.BlockSpec((1, RWKV_DIM), lambda b, i: (0, 0))
    return pl.pallas_call(
        _rwkv_post_kernel,
        grid=(B, nt),
        in_specs=[pl.BlockSpec((tl, RWKV_DIM), lambda b, i: (i, b)), tok, tok, vec, vec,
                  pl.BlockSpec((GROUP_BLOCK, GROUP_BLOCK), lambda b, i: (0, 0))],
        out_specs=tok,
        out_shape=jax.ShapeDtypeStruct((B * L, RWKV_DIM), BF16),
        compiler_params=_cparams("parallel", "parallel"),
        name="rwkv_post",
    )(y, bonus, g, lp['lnx_g'].reshape(1, -1), lp['lnx_b'].reshape(1, -1), _group_ones())


def _qk_norm_kernel(qk_ref, v_ref, qg_ref, kg_ref, ones_ref, kout_ref, qb_ref, kb_ref, vb_ref, *, length):
    tl = qk_ref.shape[0]
    valid = (pl.program_id(1) * tl + lax.broadcasted_iota(jnp.int32, (tl, 1), 0)) < length
    q = qk_ref[:, :ATT_QK]
    k = qk_ref[:, ATT_QK:]
    inv_n = 1.0 / QK_DIM
    qn = q * lax.rsqrt(_gsum(q * q, ones_ref) * inv_n + RMS_EPS) * qg_ref[...]
    kn = k * lax.rsqrt(_gsum(k * k, ones_ref) * inv_n + RMS_EPS) * kg_ref[...]
    kout_ref[...] = kn
    qb_ref[...] = jnp.where(valid, qn * Q_SCALE, 0.0).astype(BF16)
    kb_ref[...] = jnp.where(valid, kn, 0.0).astype(BF16)
    vb_ref[...] = jnp.where(valid, v_ref[...], 0.0).astype(BF16)


def _qk_norm(pqk, pv, lp, lpad, tl):
    B, L, _ = pqk.shape
    tile_g = lambda g: jnp.tile(g.astype(F32), ATT_QK // QK_DIM).reshape(1, ATT_QK)
    blk = lambda n: pl.BlockSpec((None, tl, n), lambda b, i: (b, i, 0))
    vec = pl.BlockSpec((1, ATT_QK), lambda b, i: (0, 0))
    bshape = jax.ShapeDtypeStruct((B, lpad, ATT_QK), BF16)
    return pl.pallas_call(
        functools.partial(_qk_norm_kernel, length=L),
        grid=(B, lpad // tl),
        in_specs=[blk(2 * ATT_QK), blk(ATT_DIM), vec, vec,
                  pl.BlockSpec((GROUP_BLOCK, GROUP_BLOCK), lambda b, i: (0, 0))],
        out_specs=[blk(ATT_QK)] * 4,
        out_shape=[jax.ShapeDtypeStruct((B, L, ATT_QK), F32), bshape, bshape, bshape],
        compiler_params=_cparams("parallel", "parallel"),
        name="qk_norm",
    )(pqk, pv, tile_g(lp['q_norm_g']), tile_g(lp['k_norm_g']), _group_ones())


def _attn_kernel(slope_ref, lam_ref, q_ref, k_ref, v_ref, g_ref, o_ref,
                 qt_sc, vt_sc, rel_sc, m_sc, l_sc, a_sc,
                 *, chunked, q_pos0, n_keys, out_scale):
    tq = q_ref.shape[0]
    n_tiles, _, tk = vt_sc.shape
    h = pl.program_id(1)
    i = pl.program_id(2)

    @pl.when(i == 0)
    def _():
        def transpose_tile(j, carry):
            k0 = pl.multiple_of(j * tk, tk)
            vt_sc[j] = v_ref[pl.ds(k0, tk), :].astype(F32).T.astype(BF16)
            return carry
        lax.fori_loop(0, n_tiles, transpose_tile, 0)

    qt = q_ref[...].astype(F32).T
    sub = lax.broadcasted_iota(jnp.int32, qt.shape, 0)
    qt_sc[0] = jnp.where(sub < QK_DIM, qt, 0.0).astype(BF16)
    qt_sc[1] = jnp.where(sub >= QK_DIM, qt, 0.0).astype(BF16)
    q_first = q_pos0 + i * tq
    qpos = q_first + lax.broadcasted_iota(jnp.int32, (1, tq), 1)
    if chunked:
        chunk_end = lambda p: N_META + CHUNK + (((p - N_META) >> CHUNK_SHIFT) << CHUNK_SHIFT)
        vis = jnp.where(qpos < N_META, N_META, chunk_end(qpos))
        vis_last = chunk_end(q_first + tq - 1)
        n_kt = (jnp.minimum(vis_last, n_keys) + tk - 1) // tk
    else:
        vis = jnp.full((1, tq), n_keys, jnp.int32)
        n_kt = (n_keys + tk - 1) // tk
    n_past = jnp.minimum((q_first + 1) // tk, n_kt)
    slope = slope_ref[h] * LOG2E
    rel_sc[...] = slope * (lax.broadcasted_iota(jnp.int32, (tk, tq), 1)
                           - lax.broadcasted_iota(jnp.int32, (tk, tq), 0)).astype(F32)

    m_sc[...] = jnp.full(m_sc.shape, -jnp.inf, F32)
    l_sc[...] = jnp.zeros(l_sc.shape, F32)
    a_sc[...] = jnp.zeros(a_sc.shape, F32)

    def accumulate(mi, s, shift, vtj):
        m_old = m_sc[mi]
        m_new = jnp.maximum(m_old, jnp.max(s, axis=0, keepdims=True) + shift)
        alpha = jnp.exp2(m_old - m_new)
        p = jnp.exp2(s - (m_new - shift))
        l_sc[mi] = alpha * l_sc[mi] + jnp.sum(p, axis=0, keepdims=True)
        a_sc[mi] = alpha * a_sc[mi] + jnp.dot(vtj, p.astype(BF16), preferred_element_type=F32)
        m_sc[mi] = m_new

    def scores(j):
        kj = k_ref[pl.ds(pl.multiple_of(j * tk, tk), tk), :]
        return tuple(jnp.dot(kj, qt_sc[mi], preferred_element_type=F32) for mi in range(2))

    def body(j, s_cur):
        @pl.when(j >= n_past)
        def _():
            kpos = j * tk + lax.broadcasted_iota(jnp.int32, (tk, 1), 0)
            bias = slope * jnp.abs(qpos - kpos).astype(F32)
            rel_sc[...] = jnp.where(kpos < vis, bias, -NEG_INF)

        s_next = scores(jnp.minimum(j + 1, n_kt - 1))
        shift = jnp.where(j < n_past, -slope * (q_first - j * tk).astype(F32), 0.0)
        vtj = vt_sc[j]
        pen = rel_sc[...]
        for mi in range(2):
            accumulate(mi, s_cur[mi] - pen, shift, vtj)
        return s_next

    lax.fori_loop(0, n_kt, body, scores(0))

    o = a_sc[0] / l_sc[0] - lam_ref[0] * (a_sc[1] / l_sc[1])
    o = o * lax.rsqrt(jnp.mean(o * o, axis=0, keepdims=True) + RMS_EPS) * g_ref[...] * out_scale
    o_ref[...] = o.T.astype(BF16)


def _attention(qb, kb, vb, lam, subln_g, lam_init, *, chunked, q_pos0, n_keys, n_q, tq, tk):
    B = qb.shape[0]
    lk = kb.shape[1]
    slopes = jnp.asarray([2.0 ** (-8.0 * (i + 1) / ATT_HEADS) for i in range(ATT_HEADS)], F32)
    gain = jnp.broadcast_to(subln_g.astype(F32).reshape(V_DIM, 1), (V_DIM, tq))
    smem = pl.BlockSpec(memory_space=pltpu.SMEM)
    qspec = pl.BlockSpec((None, tq, V_DIM), lambda b, h, i: (b, i, h))
    kspec = pl.BlockSpec((None, lk, V_DIM), lambda b, h, i: (b, 0, h))
    return pl.pallas_call(
        functools.partial(_attn_kernel, chunked=chunked, q_pos0=q_pos0, n_keys=n_keys,
                          out_scale=1.0 - lam_init),
        grid=(B, ATT_HEADS, n_q),
        in_specs=[smem, smem, qspec, kspec, kspec, pl.BlockSpec((V_DIM, tq), lambda b, h, i: (0, 0))],
        out_specs=qspec,
        out_shape=jax.ShapeDtypeStruct((B, n_q * tq, ATT_DIM), BF16),
        scratch_shapes=[pltpu.VMEM((2, V_DIM, tq), BF16), pltpu.VMEM((lk // tk, V_DIM, tk), BF16),
                        pltpu.VMEM((tk, tq), F32),
                        pltpu.VMEM((2, 1, tq), F32), pltpu.VMEM((2, 1, tq), F32),
                        pltpu.VMEM((2, V_DIM, tq), F32)],
        compiler_params=_cparams("parallel", "parallel", "arbitrary"),
        name="diff_attention",
    )(slopes, lam.reshape(1).astype(F32), qb, kb, vb, gain)


def _merge_kernel(or_ref, oa_ref, pg_ref, x_ref, wr_ref, wa_ref, wo_ref, g2_ref, rw_ref, rb_ref,
                  x1_ref, h2_ref, lg_ref):
    gates = jax.nn.sigmoid(pg_ref[...])
    br = jnp.dot(or_ref[...], wr_ref[...], preferred_element_type=F32)
    ba = jnp.dot(oa_ref[...], wa_ref[...], preferred_element_type=F32)
    merged = gates[:, :D_MODEL] * br + gates[:, D_MODEL:] * ba
    x1 = x_ref[...] + jnp.dot(merged.astype(BF16), wo_ref[...], preferred_element_type=F32)
    h2 = x1 * lax.rsqrt(jnp.mean(x1 * x1, axis=-1, keepdims=True) + RMS_EPS) * g2_ref[...]
    x1_ref[...] = x1
    h2_ref[...] = h2.astype(BF16)
    lg_ref[...] = jnp.dot(h2, rw_ref[...], preferred_element_type=F32,
                          precision=lax.Precision.HIGHEST) + rb_ref[...]


def _merge(o_r, o_a, pg, x, lp, tm):
    T = x.shape[0]
    rw = jnp.zeros((D_MODEL, LANES), F32).at[:, :N_EXPERTS].set(lp['router_w'].astype(F32))
    rb = jnp.zeros((1, LANES), F32).at[0, :N_EXPERTS].set(lp['router_b'].astype(F32))
    tok = lambda n: pl.BlockSpec((tm, n), lambda i: (i, 0))
    mat = lambda m, n: pl.BlockSpec((m, n), lambda i: (0, 0))
    return pl.pallas_call(
        _merge_kernel,
        grid=(T // tm,),
        in_specs=[tok(RWKV_DIM), tok(ATT_DIM), tok(2 * D_MODEL), tok(D_MODEL),
                  mat(RWKV_DIM, D_MODEL), mat(ATT_DIM, D_MODEL), mat(D_MODEL, D_MODEL),
                  mat(1, D_MODEL), mat(D_MODEL, LANES), mat(1, LANES)],
        out_specs=[tok(D_MODEL), tok(D_MODEL), tok(LANES)],
        out_shape=[jax.ShapeDtypeStruct((T, D_MODEL), F32), jax.ShapeDtypeStruct((T, D_MODEL), BF16),
                   jax.ShapeDtypeStruct((T, LANES), F32)],
        compiler_params=_cparams("parallel"),
        name="merge_router",
    )(o_r, o_a, pg, x, lp['w_br_r'].astype(BF16), lp['w_br_a'].astype(BF16), lp['w_out'].astype(BF16),
      lp['norm2_g'].reshape(1, -1), rw, rb)


def _moe_kernel(be_ref, nact_ref, x_ref, gate_ref, wgu_ref, bgu_ref, wdn_ref, bdn_ref, o_ref):
    i = pl.program_id(0)

    @pl.when(i < nact_ref[0])
    def _():
        gu = jnp.dot(x_ref[...], wgu_ref[...], preferred_element_type=F32) + bgu_ref[...]
        d_ff = gu.shape[1] // 2
        x_glu = jnp.minimum(gu[:, :d_ff], SWIGLU_LIMIT)
        x_lin = jnp.clip(gu[:, d_ff:], -SWIGLU_LIMIT, SWIGLU_LIMIT)
        act = x_glu * jax.nn.sigmoid(SWIGLU_ALPHA * x_glu) * (x_lin + 1.0)
        out = jnp.dot(act.astype(BF16), wdn_ref[...], preferred_element_type=F32) + bdn_ref[...]
        o_ref[...] = out * gate_ref[...]

    @pl.when(i >= nact_ref[0])
    def _():
        o_ref[...] = jnp.zeros_like(o_ref)


def _moe_ffn_blocks(xs, row_gate, blk_e, n_act, w_gu, b_gu, w_dn, b_dn):
    n_rows, D = xs.shape
    n_blk = n_rows // MOE_BLOCK
    d_ff2 = w_gu.shape[2]
    grid_spec = pltpu.PrefetchScalarGridSpec(
        num_scalar_prefetch=2,
        grid=(n_blk,),
        in_specs=[pl.BlockSpec((MOE_BLOCK, D), lambda i, be, na: (i, 0)),
                  pl.BlockSpec((MOE_BLOCK, 1), lambda i, be, na: (i, 0)),
                  pl.BlockSpec((None, D, d_ff2), lambda i, be, na: (be[i], 0, 0)),
                  pl.BlockSpec((None, 1, d_ff2), lambda i, be, na: (be[i], 0, 0)),
                  pl.BlockSpec((None, d_ff2 // 2, D), lambda i, be, na: (be[i], 0, 0)),
                  pl.BlockSpec((None, 1, D), lambda i, be, na: (be[i], 0, 0))],
        out_specs=pl.BlockSpec((MOE_BLOCK, D), lambda i, be, na: (i, 0)),
    )
    return pl.pallas_call(
        _moe_kernel,
        grid_spec=grid_spec,
        out_shape=jax.ShapeDtypeStruct((n_rows, D), F32),
        compiler_params=_cparams("arbitrary"),
        name="moe_ffn",
    )(blk_e, n_act, xs, row_gate.reshape(n_rows, 1), w_gu, b_gu.reshape(N_EXPERTS, 1, d_ff2),
      w_dn, b_dn.reshape(N_EXPERTS, 1, D))


def _moe(h2, logits, x1, moe_w):
    T, D = h2.shape
    top_v, top_e = lax.top_k(logits, TOP_K)
    gate = jax.nn.softmax(top_v, axis=-1)
    rows = T * TOP_K
    flat_e = top_e.reshape(-1).astype(jnp.int32)
    order = jnp.argsort(flat_e).astype(jnp.int32)
    rank = jnp.argsort(order).astype(jnp.int32)
    counts = jnp.sum(flat_e[:, None] == jnp.arange(N_EXPERTS, dtype=jnp.int32)[None, :], axis=0,
                     dtype=jnp.int32)
    padded = (counts + MOE_BLOCK - 1) // MOE_BLOCK * MOE_BLOCK
    pad_end = jnp.cumsum(padded)
    pad_start = pad_end - padded
    start = jnp.cumsum(counts) - counts
    n_blk = -(-(rows + N_EXPERTS * (MOE_BLOCK - 1)) // MOE_BLOCK)
    n_rows = n_blk * MOE_BLOCK
    blk_start = jnp.arange(n_blk, dtype=jnp.int32) * MOE_BLOCK
    blk_e = jnp.minimum(jnp.searchsorted(pad_end, blk_start, side='right'), N_EXPERTS - 1).astype(jnp.int32)
    n_act = (pad_end[-1:] // MOE_BLOCK).astype(jnp.int32)
    row_e = jnp.repeat(blk_e, MOE_BLOCK)
    row_off = jnp.arange(n_rows, dtype=jnp.int32) - pad_start[row_e]
    row_valid = row_off < counts[row_e]
    row_flat = order[jnp.where(row_valid, start[row_e] + row_off, 0)]
    row_tok = jnp.where(row_valid, row_flat // TOP_K, T)
    row_gate = jnp.where(row_valid, gate.reshape(-1)[row_flat], 0.0)
    pos = (pad_start[flat_e] + rank - start[flat_e]).reshape(T, TOP_K)

    h_pad = jnp.concatenate([h2, jnp.zeros((1, D), h2.dtype)], axis=0)
    xs = h_pad[row_tok]
    out = _moe_ffn_blocks(xs, row_gate, blk_e, n_act, *moe_w)
    return x1 + jnp.sum(out[pos], axis=1)


def _to_chains(a, L, nc):
    return jnp.swapaxes(a.reshape(L, nc, RWKV_HEAD), 1, 2)


def _trunk_layer(x, lp, w_bf, moe_w, lam, lam_init, s0, prev_row, cache=None):
    B, L, D = x.shape
    T = B * L
    x2 = x.reshape(T, D)
    tl = _tile(L, 320, BF16_ROWS)
    tm = _tile(T, 1024, BF16_ROWS) if cache is not None else _tile(L, 1024, BF16_ROWS)
    tm_small = _tile(tm, 320, BF16_ROWS)
    w_r, w_qk, w_v, w_g = w_bf

    pr = _norm_matmul(x2, lp['norm1_g'], w_r, tm_small)
    pqk = _norm_matmul(x2, lp['norm1_g'], w_qk, tm)
    pv = _norm_matmul(x2, lp['norm1_g'], w_v, tm)
    pg = _norm_matmul(x2, lp['norm1_g'], w_g, tm)

    pr3 = pr.reshape(B, L, RWKV_PROJ)
    tile_last = pr3.reshape(B, L // tl, tl, RWKV_PROJ)[:, :-1, -1]
    prev_rows = jnp.concatenate([prev_row.astype(F32), tile_last], axis=1).reshape(T // tl, 1, RWKV_PROJ)
    *scan_ops, bonus, g = _rwkv_prep(pr, prev_rows, lp, B, L, tl)
    nc = B * RWKV_HEADS
    ops = [_to_chains(a, L, nc) for a in scan_ops]
    s0c = s0.astype(F32).transpose(2, 3, 0, 1).reshape(RWKV_HEAD, RWKV_HEAD, nc)
    y_c, s_fin_c = _rwkv_scan(ops, s0c, _tile(L, SCAN_TT, 1))
    y = jnp.swapaxes(y_c, 1, 2).reshape(L, B * RWKV_DIM)
    s_fin = s_fin_c.reshape(RWKV_HEAD, RWKV_HEAD, B, RWKV_HEADS).transpose(2, 3, 0, 1)
    o_r = _rwkv_post(y, bonus, g, lp, B, L, tl)

    if cache is None:
        assert ATT_TK % ATT_TQ == 0
        lpad = -(-L // ATT_TK) * ATT_TK
        k_rows, qb, kb, vb = _qk_norm(pqk.reshape(B, L, 2 * ATT_QK), pv.reshape(B, L, ATT_DIM), lp, lpad,
                                      _tile(lpad, 512, BF16_ROWS))
        o_a = _attention(qb, kb, vb, lam, lp['subln_g'], lam_init, chunked=True, q_pos0=0, n_keys=L,
                         n_q=-(-L // ATT_TQ), tq=ATT_TQ, tk=ATT_TK)
    else:
        cache_k, cache_v = cache
        past = cache_k.shape[1]
        k_rows, qb, kb, vb = _qk_norm(pqk.reshape(B, L, 2 * ATT_QK), pv.reshape(B, L, ATT_DIM), lp, L, L)
        n_keys = past + L
        lk_pad = -(-n_keys // ATT_TK) * ATT_TK
        zk = jnp.zeros((B, lk_pad - n_keys, ATT_QK), BF16)
        k_all = jnp.concatenate([cache_k.reshape(B, past, ATT_QK).astype(BF16), kb, zk], axis=1)
        v_all = jnp.concatenate([cache_v.reshape(B, past, ATT_DIM).astype(BF16), vb, zk], axis=1)
        tq = -(-L // LANES) * LANES
        qb = jnp.concatenate([qb, jnp.zeros((B, tq - L, ATT_QK), BF16)], axis=1)
        o_a = _attention(qb, k_all, v_all, lam, lp['subln_g'], lam_init, chunked=False, q_pos0=past,
                         n_keys=n_keys, n_q=1, tq=tq, tk=ATT_TK)
    o_a = o_a[:, :L].reshape(T, ATT_DIM)

    x1, h2, logits = _merge(o_r, o_a, pg, x2, lp, tm_small)
    x_out = _moe(h2, logits[:, :N_EXPERTS], x1, moe_w)
    return (x_out.reshape(B, L, D), k_rows.reshape(B, L, ATT_HEADS, 2, QK_DIM),
            pv.reshape(B, L, ATT_HEADS, V_DIM), s_fin, pr3[:, -1:])


def kernel(x_prompt, x_sample, cache_k, cache_v, state_wkv, state_shift, meta, norm1_g, w_in, shift_mu, w0, w_up, a0, a_up, g_up, k_k, k_a, r_k, lnx_g, lnx_b, q_norm_g, k_norm_g, lam_q1, lam_k1, lam_q2, lam_k2, subln_g, w_br_r, w_br_a, w_out, norm2_g, router_w, router_b, w_gu, b_gu, w_dn, b_dn):
    depth = w_in.shape[0]
    n_prompt = x_prompt.shape[0]
    meta_rows = jnp.broadcast_to(meta.astype(x_prompt.dtype)[None], (n_prompt, N_META, D_MODEL))
    xp = jnp.concatenate([meta_rows, x_prompt], axis=1)
    xs = x_sample
    st_p, st_s = [], []
    for l in range(depth):
        lp = dict(norm1_g=norm1_g[l], shift_mu=shift_mu[l], w0=w0[l], w_up=w_up[l],
                  a0=a0[l], a_up=a_up[l], g_up=g_up[l], k_k=k_k[l], k_a=k_a[l], r_k=r_k[l].reshape(-1),
                  lnx_g=lnx_g[l], lnx_b=lnx_b[l], q_norm_g=q_norm_g[l], k_norm_g=k_norm_g[l],
                  subln_g=subln_g[l], w_br_r=w_br_r[l], w_br_a=w_br_a[l], w_out=w_out[l],
                  norm2_g=norm2_g[l], router_w=router_w[l], router_b=router_b[l])
        wl = w_in[l].astype(BF16)
        c0, c1, c2 = RWKV_PROJ, RWKV_PROJ + 2 * ATT_QK, RWKV_PROJ + 2 * ATT_QK + ATT_DIM
        w_bf = (wl[:, :c0], wl[:, c0:c1], wl[:, c1:c2], wl[:, c2:])
        moe_w = (w_gu[l].astype(BF16), b_gu[l].astype(F32), w_dn[l].astype(BF16), b_dn[l].astype(F32))
        lam_init = 0.8 - 0.6 * math.exp(-0.3 * l)
        lam = (jnp.exp(jnp.sum(lam_q1[l].astype(F32) * lam_k1[l].astype(F32)))
               - jnp.exp(jnp.sum(lam_q2[l].astype(F32) * lam_k2[l].astype(F32))) + lam_init)
        s0 = jnp.zeros((n_prompt, RWKV_HEADS, RWKV_HEAD, RWKV_HEAD), F32)
        r0 = jnp.zeros((n_prompt, 1, RWKV_PROJ), F32)
        xp, kp, vp, sp, rp = _trunk_layer(xp, lp, w_bf, moe_w, lam, lam_init, s0, r0)
        xs, ks, vs, ss, rs = _trunk_layer(xs, lp, w_bf, moe_w, lam, lam_init, state_wkv[l], state_shift[l],
                                          cache=(cache_k[l], cache_v[l]))
        st_p.append((kp, vp, sp, rp))
        st_s.append((ks, vs, ss, rs))
    stack = lambda st, j: jnp.stack([t[j] for t in st])
    return (xp[:, N_META:], xs,
            stack(st_p, 0), stack(st_p, 1), stack(st_p, 2), stack(st_p, 3),
            stack(st_s, 0), stack(st_s, 1), stack(st_s, 2), stack(st_s, 3))
```

```python
import functools
import math

import jax
import jax.numpy as jnp
import numpy as np
from jax import lax
from jax.experimental import pallas as pl
from jax.experimental.pallas import tpu as pltpu

F32 = jnp.float32
BF16 = jnp.bfloat16

D_MODEL = 1024
CHUNK = 64
CHUNK_SHIFT = CHUNK.bit_length() - 1
assert 1 << CHUNK_SHIFT == CHUNK
N_META = 16
RWKV_HEAD = 64
RWKV_HEADS = D_MODEL // RWKV_HEAD
RWKV_DIM = RWKV_HEADS * RWKV_HEAD
DECAY_LORA = 64
AAA_LORA = 64
GATE_LORA = 128
RWKV_PROJ = 3 * RWKV_DIM + DECAY_LORA + AAA_LORA + GATE_LORA
LNX_EPS = 64e-5
ATT_HEADS = 8
QK_DIM = 64
V_DIM = 2 * QK_DIM
ATT_QK = ATT_HEADS * 2 * QK_DIM
ATT_DIM = ATT_HEADS * V_DIM
N_EXPERTS = 32
TOP_K = 4
SWIGLU_ALPHA = 1.702
SWIGLU_LIMIT = 7.0
RMS_EPS = 1e-6
NEG_INF = -1e30
LOG2E = math.log2(math.e)
Q_SCALE = QK_DIM ** -0.5 * LOG2E

LANES = 128
SUBLANES = 8
BF16_ROWS = 16
MXU_DIM = 256
VMEM_LIMIT = 56 * 1024 * 1024

GROUP_BLOCK = MXU_DIM
MOE_BLOCK = 256
ATT_TQ = 256
ATT_TK = 512
SCAN_TT = 24
SCAN_UNROLL = 8


def _cparams(*sem):
    return pltpu.CompilerParams(dimension_semantics=sem, vmem_limit_bytes=VMEM_LIMIT)


def _tile(n, target, mult):
    best = None
    for t in range(mult, min(n, target) + 1, mult):
        if n % t == 0:
            best = t
    return best if best is not None else n


def _group_ones():
    g = np.arange(GROUP_BLOCK) // RWKV_HEAD
    return jnp.asarray((g[:, None] == g[None, :]).astype(np.float32), BF16)


def _gsum(x, ones_ref):
    hi = x.astype(BF16)
    lo = (x - hi.astype(F32)).astype(BF16)
    ones = ones_ref[...]
    outs = []
    for c in range(x.shape[1] // GROUP_BLOCK):
        sl = slice(c * GROUP_BLOCK, (c + 1) * GROUP_BLOCK)
        outs.append(jnp.dot(hi[:, sl], ones, preferred_element_type=F32)
                    + jnp.dot(lo[:, sl], ones, preferred_element_type=F32))
    return jnp.concatenate(outs, axis=1)


def _norm_matmul_kernel(x_ref, g_ref, w_ref, o_ref):
    x = x_ref[...]
    h = x * lax.rsqrt(jnp.mean(x * x, axis=-1, keepdims=True) + RMS_EPS) * g_ref[...]
    o_ref[...] = jnp.dot(h.astype(BF16), w_ref[...], preferred_element_type=F32)


def _norm_matmul(x, g, w, tm):
    T, D = x.shape
    N = w.shape[1]
    return pl.pallas_call(
        _norm_matmul_kernel,
        grid=(T // tm,),
        in_specs=[pl.BlockSpec((tm, D), lambda i: (i, 0)),
                  pl.BlockSpec((1, D), lambda i: (0, 0)),
                  pl.BlockSpec((D, N), lambda i: (0, 0))],
        out_specs=pl.BlockSpec((tm, N), lambda i: (i, 0)),
        out_shape=jax.ShapeDtypeStruct((T, N), F32),
        compiler_params=_cparams("parallel"),
        name="norm_matmul",
    )(x, g.reshape(1, D), w)


def _rwkv_prep_kernel(pr_ref, prev_ref, mu_ref, w0_ref, wup_ref, a0_ref, aup_ref, gup_ref,
                      kk_ref, ka_ref, rk_ref, ones_ref,
                      kn_o, d_o, b_o, k_o, v_o, r_o, bonus_o, g_o):
    pr = pr_ref[...]
    row = lax.broadcasted_iota(jnp.int32, pr.shape, 0)
    prev = jnp.where(row == 0, prev_ref[...], pltpu.roll(pr, 1, axis=0))
    xs = pr + (prev - pr) * mu_ref[...]
    r = xs[:, :RWKV_DIM]
    k = xs[:, RWKV_DIM:2 * RWKV_DIM]
    v = xs[:, 2 * RWKV_DIM:3 * RWKV_DIM]
    lora_in = xs[:, 3 * RWKV_DIM:3 * RWKV_DIM + DECAY_LORA + AAA_LORA]
    gd = xs[:, 3 * RWKV_DIM + DECAY_LORA + AAA_LORA:]
    wl = jnp.dot(jnp.tanh(lora_in).astype(BF16), wup_ref[...], preferred_element_type=F32)
    al = jnp.dot(lora_in.astype(BF16), aup_ref[...], preferred_element_type=F32)
    z = -(w0_ref[...] + wl)
    w = -(jnp.maximum(z, 0.0) + jnp.log1p(jnp.exp(-jnp.abs(z)))) - 0.5
    decay = jnp.exp(-jnp.exp(w))
    a = jax.nn.sigmoid(a0_ref[...] + al)
    g = jnp.dot(jax.nn.sigmoid(gd).astype(BF16), gup_ref[...], preferred_element_type=F32)
    kk = k * kk_ref[...]
    nrm = jnp.sqrt(_gsum(kk * kk, ones_ref))
    kk = kk / jnp.maximum(nrm, 1e-12)
    k = k * (1.0 + (a - 1.0) * ka_ref[...])
    kn_o[...] = -kk
    d_o[...] = decay
    b_o[...] = kk * a
    k_o[...] = k
    v_o[...] = v
    r_o[...] = r
    bonus_o[...] = _gsum(r * k * rk_ref[...], ones_ref) * v
    g_o[...] = g


def _rwkv_prep(pr, prev_rows, lp, B, L, tl):
    nt = L // tl
    row = lambda a: a.reshape(1, -1).astype(F32)
    zpad = jnp.zeros((DECAY_LORA, RWKV_DIM), F32)
    wup = jnp.concatenate([lp['w_up'], zpad], axis=0).astype(BF16)
    aup = jnp.concatenate([zpad, lp['a_up']], axis=0).astype(BF16)
    vec = lambda n: pl.BlockSpec((1, n), lambda b, i: (0, 0))
    mat = lambda m, n: pl.BlockSpec((m, n), lambda b, i: (0, 0))
    tok = pl.BlockSpec((tl, RWKV_DIM), lambda b, i: (b * nt + i, 0))
    time_major = pl.BlockSpec((tl, RWKV_DIM), lambda b, i: (i, b))
    return pl.pallas_call(
        _rwkv_prep_kernel,
        grid=(B, nt),
        in_specs=[pl.BlockSpec((tl, RWKV_PROJ), lambda b, i: (b * nt + i, 0)),
                  pl.BlockSpec((None, 1, RWKV_PROJ), lambda b, i: (b * nt + i, 0, 0)),
                  vec(RWKV_PROJ), vec(RWKV_DIM), mat(2 * DECAY_LORA, RWKV_DIM), vec(RWKV_DIM),
                  mat(2 * AAA_LORA, RWKV_DIM), mat(GATE_LORA, RWKV_DIM),
                  vec(RWKV_DIM), vec(RWKV_DIM), vec(RWKV_DIM), mat(GROUP_BLOCK, GROUP_BLOCK)],
        out_specs=[time_major] * 6 + [tok] * 2,
        out_shape=[jax.ShapeDtypeStruct((L, B * RWKV_DIM), F32)] * 6
                  + [jax.ShapeDtypeStruct((B * L, RWKV_DIM), F32)] * 2,
        compiler_params=_cparams("parallel", "parallel"),
        name="rwkv_prep",
    )(pr, prev_rows, row(lp['shift_mu']), row(lp['w0']), wup, row(lp['a0']), aup,
      lp['g_up'].astype(BF16), row(lp['k_k']), row(lp['k_a']), row(lp['r_k']), _group_ones())


def _rwkv_scan_kernel(kn_ref, d_ref, b_ref, k_ref, v_ref, r_ref, s0_ref, y_ref, sfin_ref,
                      s_sc, opa_sc, opb_sc, yt_sc):
    tb = pl.program_id(1)

    @pl.when(tb == 0)
    def _():
        s_sc[...] = s0_ref[...]

    tt = kn_ref.shape[0]
    refs = (kn_ref, d_ref, b_ref, k_ref, v_ref, r_ref)

    def stage(t, op_sc):
        t = jnp.minimum(t, tt - 1)
        for i, ref in enumerate(refs):
            op_sc[i] = ref[t].T

    def step(t, op_sc):
        kn, d, bb, kk, rr = op_sc[0], op_sc[1], op_sc[2], op_sc[3], op_sc[5]

        def vstep(vi, c):
            sv = s_sc[vi]
            sa = jnp.sum(sv * kn, axis=0, keepdims=True)
            vv = op_sc[4, pl.ds(vi, 1), :]
            sn = sv * d + sa * bb + vv * kk
            s_sc[vi] = sn
            yt_sc[pl.ds(vi, 1), :] = jnp.sum(sn * rr, axis=0, keepdims=True)
            return c

        lax.fori_loop(0, RWKV_HEAD, vstep, 0, unroll=True)
        y_ref[t] = yt_sc[...].T

    stage(0, opa_sc)

    def pair(i, carry):
        t0 = 2 * i
        stage(t0 + 1, opb_sc)
        step(t0, opa_sc)
        stage(t0 + 2, opa_sc)
        step(t0 + 1, opb_sc)
        return carry

    lax.fori_loop(0, tt // 2, pair, 0)

    @pl.when(tb == pl.num_programs(1) - 1)
    def _():
        sfin_ref[...] = s_sc[...]


def _rwkv_scan(ops, s0, tt):
    L, nc, n = ops[0].shape
    assert tt % 2 == 0 and L % tt == 0 and nc % LANES == 0
    tspec = pl.BlockSpec((tt, LANES, n), lambda c, t: (t, c, 0))
    sspec = pl.BlockSpec((n, n, LANES), lambda c, t: (0, 0, c))
    return pl.pallas_call(
        _rwkv_scan_kernel,
        grid=(nc // LANES, L // tt),
        in_specs=[tspec] * 6 + [sspec],
        out_specs=[tspec, sspec],
        out_shape=[jax.ShapeDtypeStruct((L, nc, n), F32), jax.ShapeDtypeStruct((n, n, nc), F32)],
        scratch_shapes=[pltpu.VMEM((n, n, LANES), F32), pltpu.VMEM((6, n, LANES), F32),
                        pltpu.VMEM((6, n, LANES), F32), pltpu.VMEM((n, LANES), F32)],
        compiler_params=_cparams("parallel", "arbitrary"),
        name="rwkv_scan",
    )(*ops, s0)


def _rwkv_post_kernel(y_ref, bonus_ref, g_ref, lg_ref, lb_ref, ones_ref, o_ref):
    y = y_ref[...]
    inv_n = 1.0 / RWKV_HEAD
    mean = _gsum(y, ones_ref) * inv_n
    yc = y - mean
    var = _gsum(yc * yc, ones_ref) * inv_n
    yn = yc * lax.rsqrt(var + LNX_EPS) * lg_ref[...] + lb_ref[...]
    o_ref[...] = ((yn + bonus_ref[...]) * g_ref[...]).astype(BF16)


def _rwkv_post(y, bonus, g, lp, B, L, tl):
    nt = L // tl
    tok = pl.BlockSpec((tl, RWKV_DIM), lambda b, i: (b * nt + i, 0))
    vec = pl.BlockSpec((1, RWKV_DIM), lambda b, i: (0, 0))
    return pl.pallas_call(
        _rwkv_post_kernel,
        grid=(B, nt),
        in_specs=[pl.BlockSpec((tl, RWKV_DIM), lambda b, i: (i, b)), tok, tok, vec, vec,
                  pl.BlockSpec((GROUP_BLOCK, GROUP_BLOCK), lambda b, i: (0, 0))],
        out_specs=tok,
        out_shape=jax.ShapeDtypeStruct((B * L, RWKV_DIM), BF16),
        compiler_params=_cparams("parallel", "parallel"),
        name="rwkv_post",
    )(y, bonus, g, lp['lnx_g'].reshape(1, -1), lp['lnx_b'].reshape(1, -1), _group_ones())


def _qk_norm_kernel(qk_ref, v_ref, qg_ref, kg_ref, ones_ref, kout_ref, qb_ref, kb_ref, vb_ref, *, length):
    tl = qk_ref.shape[0]
    valid = (pl.program_id(1) * tl + lax.broadcasted_iota(jnp.int32, (tl, 1), 0)) < length
    q = qk_ref[:, :ATT_QK]
    k = qk_ref[:, ATT_QK:]
    inv_n = 1.0 / QK_DIM
    qn = q * lax.rsqrt(_gsum(q * q, ones_ref) * inv_n + RMS_EPS) * qg_ref[...]
    kn = k * lax.rsqrt(_gsum(k * k, ones_ref) * inv_n + RMS_EPS) * kg_ref[...]
    kout_ref[...] = kn
    qb_ref[...] = jnp.where(valid, qn * Q_SCALE, 0.0).astype(BF16)
    kb_ref[...] = jnp.where(valid, kn, 0.0).astype(BF16)
    vb_ref[...] = jnp.where(valid, v_ref[...], 0.0).astype(BF16)


def _qk_norm(pqk, pv, lp, lpad, tl):
    B, L, _ = pqk.shape
    tile_g = lambda g: jnp.tile(g.astype(F32), ATT_QK // QK_DIM).reshape(1, ATT_QK)
    blk = lambda n: pl.BlockSpec((None, tl, n), lambda b, i: (b, i, 0))
    vec = pl.BlockSpec((1, ATT_QK), lambda b, i: (0, 0))
    bshape = jax.ShapeDtypeStruct((B, lpad, ATT_QK), BF16)
    return pl.pallas_call(
        functools.partial(_qk_norm_kernel, length=L),
        grid=(B, lpad // tl),
        in_specs=[blk(2 * ATT_QK), blk(ATT_DIM), vec, vec,
                  pl.BlockSpec((GROUP_BLOCK, GROUP_BLOCK), lambda b, i: (0, 0))],
        out_specs=[blk(ATT_QK)] * 4,
        out_shape=[jax.ShapeDtypeStruct((B, L, ATT_QK), F32), bshape, bshape, bshape],
        compiler_params=_cparams("parallel", "parallel"),
        name="qk_norm",
    )(pqk, pv, tile_g(lp['q_norm_g']), tile_g(lp['k_norm_g']), _group_ones())


def _attn_kernel(slope_ref, lam_ref, q_ref, k_ref, v_ref, g_ref, o_ref,
                 qt_sc, vt_sc, rel_sc, sa_sc, sb_sc, m_sc, l_sc, a_sc,
                 *, chunked, q_pos0, n_keys, out_scale):
    tq = q_ref.shape[0]
    n_tiles, _, tk = vt_sc.shape
    h = pl.program_id(1)
    i = pl.program_id(2)

    @pl.when(i == 0)
    def _():
        def transpose_tile(j, carry):
            k0 = pl.multiple_of(j * tk, tk)
            vt_sc[j] = v_ref[pl.ds(k0, tk), :].astype(F32).T.astype(BF16)
            return carry
        lax.fori_loop(0, n_tiles, transpose_tile, 0)

    qt = q_ref[...].astype(F32).T
    sub = lax.broadcasted_iota(jnp.int32, qt.shape, 0)
    qt_sc[0] = jnp.where(sub < QK_DIM, qt, 0.0).astype(BF16)
    qt_sc[1] = jnp.where(sub >= QK_DIM, qt, 0.0).astype(BF16)
    q_first = q_pos0 + i * tq
    qpos = q_first + lax.broadcasted_iota(jnp.int32, (1, tq), 1)
    if chunked:
        chunk_end = lambda p: N_META + CHUNK + (((p - N_META) >> CHUNK_SHIFT) << CHUNK_SHIFT)
        vis = jnp.where(qpos < N_META, N_META, chunk_end(qpos))
        vis_last = chunk_end(q_first + tq - 1)
        n_kt = (jnp.minimum(vis_last, n_keys) + tk - 1) // tk
    else:
        vis = jnp.full((1, tq), n_keys, jnp.int32)
        n_kt = (n_keys + tk - 1) // tk
    n_past = jnp.minimum((q_first + 1) // tk, n_kt)
    slope = slope_ref[h] * LOG2E
    rel_sc[...] = slope * (lax.broadcasted_iota(jnp.int32, (tk, tq), 1)
                           - lax.broadcasted_iota(jnp.int32, (tk, tq), 0)).astype(F32)

    m_sc[...] = jnp.full(m_sc.shape, -jnp.inf, F32)
    l_sc[...] = jnp.zeros(l_sc.shape, F32)
    a_sc[...] = jnp.zeros(a_sc.shape, F32)

    def accumulate(mi, s, shift, vtj):
        m_old = m_sc[mi]
        m_new = jnp.maximum(m_old, jnp.max(s, axis=0, keepdims=True) + shift)
        alpha = jnp.exp2(m_old - m_new)
        p = jnp.exp2(s - (m_new - shift))
        l_sc[mi] = alpha * l_sc[mi] + jnp.sum(p, axis=0, keepdims=True)
        a_sc[mi] = alpha * a_sc[mi] + jnp.dot(vtj, p.astype(BF16), preferred_element_type=F32)
        m_sc[mi] = m_new

    def scores_into(s_ref, j):
        kj = k_ref[pl.ds(pl.multiple_of(jnp.minimum(j, n_tiles - 1) * tk, tk), tk), :]
        for mi in range(2):
            s_ref[mi] = jnp.dot(kj, qt_sc[mi], preferred_element_type=F32)

    def consume(s_ref, j, shift):
        vtj = vt_sc[j]
        for mi in range(2):
            accumulate(mi, s_ref[mi] - rel_sc[...], shift, vtj)

    past_shift = lambda j: -slope * (q_first - j * tk).astype(F32)

    n_pairs = n_past // 2

    @pl.when(n_pairs > 0)
    def _():
        scores_into(sa_sc, 0)

    def pair_body(jj, carry):
        j0 = 2 * jj
        scores_into(sb_sc, j0 + 1)
        consume(sa_sc, j0, past_shift(j0))
        scores_into(sa_sc, j0 + 2)
        consume(sb_sc, j0 + 1, past_shift(j0 + 1))
        return carry

    lax.fori_loop(0, n_pairs, pair_body, 0)

    def single_body(j, carry):
        @pl.when(j >= n_past)
        def _():
            kpos = j * tk + lax.broadcasted_iota(jnp.int32, (tk, 1), 0)
            bias = slope * jnp.abs(qpos - kpos).astype(F32)
            rel_sc[...] = jnp.where(kpos < vis, bias, -NEG_INF)

        scores_into(sa_sc, j)
        consume(sa_sc, j, jnp.where(j < n_past, past_shift(j), 0.0))
        return carry

    lax.fori_loop(2 * n_pairs, n_kt, single_body, 0)

    o = a_sc[0] / l_sc[0] - lam_ref[0] * (a_sc[1] / l_sc[1])
    o = o * lax.rsqrt(jnp.mean(o * o, axis=0, keepdims=True) + RMS_EPS) * g_ref[...] * out_scale
    o_ref[...] = o.T.astype(BF16)


def _attention(qb, kb, vb, lam, subln_g, lam_init, *, chunked, q_pos0, n_keys, n_q, tq, tk):
    B = qb.shape[0]
    lk = kb.shape[1]
    slopes = jnp.asarray([2.0 ** (-8.0 * (i + 1) / ATT_HEADS) for i in range(ATT_HEADS)], F32)
    gain = jnp.broadcast_to(subln_g.astype(F32).reshape(V_DIM, 1), (V_DIM, tq))
    smem = pl.BlockSpec(memory_space=pltpu.SMEM)
    qspec = pl.BlockSpec((None, tq, V_DIM), lambda b, h, i: (b, i, h))
    kspec = pl.BlockSpec((None, lk, V_DIM), lambda b, h, i: (b, 0, h))
    return pl.pallas_call(
        functools.partial(_attn_kernel, chunked=chunked, q_pos0=q_pos0, n_keys=n_keys,
                          out_scale=1.0 - lam_init),
        grid=(B, ATT_HEADS, n_q),
        in_specs=[smem, smem, qspec, kspec, kspec, pl.BlockSpec((V_DIM, tq), lambda b, h, i: (0, 0))],
        out_specs=qspec,
        out_shape=jax.ShapeDtypeStruct((B, n_q * tq, ATT_DIM), BF16),
        scratch_shapes=[pltpu.VMEM((2, V_DIM, tq), BF16), pltpu.VMEM((lk // tk, V_DIM, tk), BF16),
                        pltpu.VMEM((tk, tq), F32),
                        pltpu.VMEM((2, tk, tq), F32), pltpu.VMEM((2, tk, tq), F32),
                        pltpu.VMEM((2, 1, tq), F32), pltpu.VMEM((2, 1, tq), F32),
                        pltpu.VMEM((2, V_DIM, tq), F32)],
        compiler_params=_cparams("parallel", "parallel", "arbitrary"),
        name="diff_attention",
    )(slopes, lam.reshape(1).astype(F32), qb, kb, vb, gain)


def _merge_kernel(or_ref, oa_ref, pg_ref, x_ref, wr_ref, wa_ref, wo_ref, g2_ref, rw_ref, rb_ref,
                  x1_ref, h2_ref, lg_ref):
    gates = jax.nn.sigmoid(pg_ref[...])
    br = jnp.dot(or_ref[...], wr_ref[...], preferred_element_type=F32)
    ba = jnp.dot(oa_ref[...], wa_ref[...], preferred_element_type=F32)
    merged = gates[:, :D_MODEL] * br + gates[:, D_MODEL:] * ba
    x1 = x_ref[...] + jnp.dot(merged.astype(BF16), wo_ref[...], preferred_element_type=F32)
    h2 = x1 * lax.rsqrt(jnp.mean(x1 * x1, axis=-1, keepdims=True) + RMS_EPS) * g2_ref[...]
    x1_ref[...] = x1
    h2_ref[...] = h2.astype(BF16)
    lg_ref[...] = jnp.dot(h2, rw_ref[...], preferred_element_type=F32,
                          precision=lax.Precision.HIGHEST) + rb_ref[...]


def _merge(o_r, o_a, pg, x, lp, tm):
    T = x.shape[0]
    rw = jnp.zeros((D_MODEL, LANES), F32).at[:, :N_EXPERTS].set(lp['router_w'].astype(F32))
    rb = jnp.zeros((1, LANES), F32).at[0, :N_EXPERTS].set(lp['router_b'].astype(F32))
    tok = lambda n: pl.BlockSpec((tm, n), lambda i: (i, 0))
    mat = lambda m, n: pl.BlockSpec((m, n), lambda i: (0, 0))
    return pl.pallas_call(
        _merge_kernel,
        grid=(T // tm,),
        in_specs=[tok(RWKV_DIM), tok(ATT_DIM), tok(2 * D_MODEL), tok(D_MODEL),
                  mat(RWKV_DIM, D_MODEL), mat(ATT_DIM, D_MODEL), mat(D_MODEL, D_MODEL),
                  mat(1, D_MODEL), mat(D_MODEL, LANES), mat(1, LANES)],
        out_specs=[tok(D_MODEL), tok(D_MODEL), tok(LANES)],
        out_shape=[jax.ShapeDtypeStruct((T, D_MODEL), F32), jax.ShapeDtypeStruct((T, D_MODEL), BF16),
                   jax.ShapeDtypeStruct((T, LANES), F32)],
        compiler_params=_cparams("parallel"),
        name="merge_router",
    )(o_r, o_a, pg, x, lp['w_br_r'].astype(BF16), lp['w_br_a'].astype(BF16), lp['w_out'].astype(BF16),
      lp['norm2_g'].reshape(1, -1), rw, rb)


def _moe_kernel(be_ref, nact_ref, x_ref, gate_ref, wgu_ref, bgu_ref, wdn_ref, bdn_ref, o_ref):
    i = pl.program_id(0)

    @pl.when(i < nact_ref[0])
    def _():
        gu = jnp.dot(x_ref[...], wgu_ref[...], preferred_element_type=F32) + bgu_ref[...]
        d_ff = gu.shape[1] // 2
        x_glu = jnp.minimum(gu[:, :d_ff], SWIGLU_LIMIT)
        x_lin = jnp.clip(gu[:, d_ff:], -SWIGLU_LIMIT, SWIGLU_LIMIT)
        act = x_glu * jax.nn.sigmoid(SWIGLU_ALPHA * x_glu) * (x_lin + 1.0)
        out = jnp.dot(act.astype(BF16), wdn_ref[...], preferred_element_type=F32) + bdn_ref[...]
        o_ref[...] = out * gate_ref[...]

    @pl.when(i >= nact_ref[0])
    def _():
        o_ref[...] = jnp.zeros_like(o_ref)


def _moe_ffn_blocks(xs, row_gate, blk_e, n_act, w_gu, b_gu, w_dn, b_dn):
    n_rows, D = xs.shape
    n_blk = n_rows // MOE_BLOCK
    d_ff2 = w_gu.shape[2]
    grid_spec = pltpu.PrefetchScalarGridSpec(
        num_scalar_prefetch=2,
        grid=(n_blk,),
        in_specs=[pl.BlockSpec((MOE_BLOCK, D), lambda i, be, na: (i, 0)),
                  pl.BlockSpec((MOE_BLOCK, 1), lambda i, be, na: (i, 0)),
                  pl.BlockSpec((None, D, d_ff2), lambda i, be, na: (be[i], 0, 0)),
                  pl.BlockSpec((None, 1, d_ff2), lambda i, be, na: (be[i], 0, 0)),
                  pl.BlockSpec((None, d_ff2 // 2, D), lambda i, be, na: (be[i], 0, 0)),
                  pl.BlockSpec((None, 1, D), lambda i, be, na: (be[i], 0, 0))],
        out_specs=pl.BlockSpec((MOE_BLOCK, D), lambda i, be, na: (i, 0)),
    )
    return pl.pallas_call(
        _moe_kernel,
        grid_spec=grid_spec,
        out_shape=jax.ShapeDtypeStruct((n_rows, D), F32),
        compiler_params=_cparams("arbitrary"),
        name="moe_ffn",
    )(blk_e, n_act, xs, row_gate.reshape(n_rows, 1), w_gu, b_gu.reshape(N_EXPERTS, 1, d_ff2),
      w_dn, b_dn.reshape(N_EXPERTS, 1, D))


def _moe(h2, logits, x1, moe_w):
    T, D = h2.shape
    top_v, top_e = lax.top_k(logits, TOP_K)
    gate = jax.nn.softmax(top_v, axis=-1)
    rows = T * TOP_K
    flat_e = top_e.reshape(-1).astype(jnp.int32)
    e_sorted, order = lax.sort((flat_e, jnp.arange(rows, dtype=jnp.int32)), num_keys=1, is_stable=True)
    rank = jnp.argsort(order).astype(jnp.int32)
    bounds = jnp.searchsorted(e_sorted, jnp.arange(N_EXPERTS + 1, dtype=jnp.int32)).astype(jnp.int32)
    start, counts = bounds[:-1], bounds[1:] - bounds[:-1]
    padded = (counts + MOE_BLOCK - 1) // MOE_BLOCK * MOE_BLOCK
    pad_end = jnp.cumsum(padded)
    pad_start = pad_end - padded
    n_blk = -(-(rows + N_EXPERTS * (MOE_BLOCK - 1)) // MOE_BLOCK)
    n_rows = n_blk * MOE_BLOCK
    blk_start = jnp.arange(n_blk, dtype=jnp.int32) * MOE_BLOCK
    blk_e = jnp.minimum(jnp.sum(blk_start[:, None] >= pad_end[None, :], axis=1, dtype=jnp.int32), N_EXPERTS - 1)
    n_act = (pad_end[-1:] // MOE_BLOCK).astype(jnp.int32)
    row_off = (blk_start - pad_start[blk_e])[:, None] + jnp.arange(MOE_BLOCK, dtype=jnp.int32)[None, :]
    row_valid = row_off < counts[blk_e][:, None]
    row_flat = order[jnp.where(row_valid, start[blk_e][:, None] + row_off, 0)]
    row_tok = jnp.where(row_valid, row_flat // TOP_K, T).reshape(n_rows)
    row_gate = jnp.where(row_valid, gate.reshape(-1)[row_flat], 0.0).reshape(n_rows)
    pos = (pad_start[flat_e] + rank - start[flat_e]).reshape(T, TOP_K)

    h_pad = jnp.concatenate([h2, jnp.zeros((1, D), h2.dtype)], axis=0)
    xs = h_pad[row_tok]
    out = _moe_ffn_blocks(xs, row_gate, blk_e, n_act, *moe_w)
    return x1 + jnp.sum(out[pos], axis=1)


def _trunk_layer(x, lp, w_bf, moe_w, lam, lam_init, s0, prev_row, cache=None):
    B, L, D = x.shape
    T = B * L
    x2 = x.reshape(T, D)
    tl = _tile(L, 320, BF16_ROWS)
    tm = _tile(T, 1024, BF16_ROWS) if cache is not None else _tile(L, 1024, BF16_ROWS)
    tm_small = _tile(tm, 320, BF16_ROWS)
    w_r, w_qk, w_v, w_g = w_bf

    pr = _norm_matmul(x2, lp['norm1_g'], w_r, tm_small)
    pqk = _norm_matmul(x2, lp['norm1_g'], w_qk, tm)
    pv = _norm_matmul(x2, lp['norm1_g'], w_v, tm)
    pg = _norm_matmul(x2, lp['norm1_g'], w_g, tm)

    pr3 = pr.reshape(B, L, RWKV_PROJ)
    tile_last = pr3.reshape(B, L // tl, tl, RWKV_PROJ)[:, :-1, -1]
    prev_rows = jnp.concatenate([prev_row.astype(F32), tile_last], axis=1).reshape(T // tl, 1, RWKV_PROJ)
    *scan_ops, bonus, g = _rwkv_prep(pr, prev_rows, lp, B, L, tl)
    nc = B * RWKV_HEADS
    ops = [a.reshape(L, nc, RWKV_HEAD) for a in scan_ops]
    s0c = s0.astype(F32).transpose(2, 3, 0, 1).reshape(RWKV_HEAD, RWKV_HEAD, nc)
    y_c, s_fin_c = _rwkv_scan(ops, s0c, _tile(L, SCAN_TT, 1))
    y = y_c.reshape(L, B * RWKV_DIM)
    s_fin = s_fin_c.reshape(RWKV_HEAD, RWKV_HEAD, B, RWKV_HEADS).transpose(2, 3, 0, 1)
    o_r = _rwkv_post(y, bonus, g, lp, B, L, tl)

    if cache is None:
        assert ATT_TK % ATT_TQ == 0
        lpad = -(-L // ATT_TK) * ATT_TK
        k_rows, qb, kb, vb = _qk_norm(pqk.reshape(B, L, 2 * ATT_QK), pv.reshape(B, L, ATT_DIM), lp, lpad,
                                      _tile(lpad, 512, BF16_ROWS))
        o_a = _attention(qb, kb, vb, lam, lp['subln_g'], lam_init, chunked=True, q_pos0=0, n_keys=L,
                         n_q=-(-L // ATT_TQ), tq=ATT_TQ, tk=ATT_TK)
    else:
        cache_k, cache_v = cache
        past = cache_k.shape[1]
        k_rows, qb, kb, vb = _qk_norm(pqk.reshape(B, L, 2 * ATT_QK), pv.reshape(B, L, ATT_DIM), lp, L, L)
        n_keys = past + L
        lk_pad = -(-n_keys // ATT_TK) * ATT_TK
        zk = jnp.zeros((B, lk_pad - n_keys, ATT_QK), BF16)
        k_all = jnp.concatenate([cache_k.reshape(B, past, ATT_QK).astype(BF16), kb, zk], axis=1)
        v_all = jnp.concatenate([cache_v.reshape(B, past, ATT_DIM).astype(BF16), vb, zk], axis=1)
        tq = -(-L // LANES) * LANES
        qb = jnp.concatenate([qb, jnp.zeros((B, tq - L, ATT_QK), BF16)], axis=1)
        o_a = _attention(qb, k_all, v_all, lam, lp['subln_g'], lam_init, chunked=False, q_pos0=past,
                         n_keys=n_keys, n_q=1, tq=tq, tk=ATT_TK)
    o_a = o_a[:, :L].reshape(T, ATT_DIM)

    x1, h2, logits = _merge(o_r, o_a, pg, x2, lp, tm_small)
    x_out = _moe(h2, logits[:, :N_EXPERTS], x1, moe_w)
    return (x_out.reshape(B, L, D), k_rows.reshape(B, L, ATT_HEADS, 2, QK_DIM),
            pv.reshape(B, L, ATT_HEADS, V_DIM), s_fin, pr3[:, -1:])


def kernel(x_prompt, x_sample, cache_k, cache_v, state_wkv, state_shift, meta, norm1_g, w_in, shift_mu, w0, w_up, a0, a_up, g_up, k_k, k_a, r_k, lnx_g, lnx_b, q_norm_g, k_norm_g, lam_q1, lam_k1, lam_q2, lam_k2, subln_g, w_br_r, w_br_a, w_out, norm2_g, router_w, router_b, w_gu, b_gu, w_dn, b_dn):
    depth = w_in.shape[0]
    n_prompt = x_prompt.shape[0]
    meta_rows = jnp.broadcast_to(meta.astype(x_prompt.dtype)[None], (n_prompt, N_META, D_MODEL))
    xp = jnp.concatenate([meta_rows, x_prompt], axis=1)
    xs = x_sample
    st_p, st_s = [], []
    for l in range(depth):
        lp = dict(norm1_g=norm1_g[l], shift_mu=shift_mu[l], w0=w0[l], w_up=w_up[l],
                  a0=a0[l], a_up=a_up[l], g_up=g_up[l], k_k=k_k[l], k_a=k_a[l], r_k=r_k[l].reshape(-1),
                  lnx_g=lnx_g[l], lnx_b=lnx_b[l], q_norm_g=q_norm_g[l], k_norm_g=k_norm_g[l],
                  subln_g=subln_g[l], w_br_r=w_br_r[l], w_br_a=w_br_a[l], w_out=w_out[l],
                  norm2_g=norm2_g[l], router_w=router_w[l], router_b=router_b[l])
        wl = w_in[l].astype(BF16)
        c0, c1, c2 = RWKV_PROJ, RWKV_PROJ + 2 * ATT_QK, RWKV_PROJ + 2 * ATT_QK + ATT_DIM
        w_bf = (wl[:, :c0], wl[:, c0:c1], wl[:, c1:c2], wl[:, c2:])
        moe_w = (w_gu[l].astype(BF16), b_gu[l].astype(F32), w_dn[l].astype(BF16), b_dn[l].astype(F32))
        lam_init = 0.8 - 0.6 * math.exp(-0.3 * l)
        lam = (jnp.exp(jnp.sum(lam_q1[l].astype(F32) * lam_k1[l].astype(F32)))
               - jnp.exp(jnp.sum(lam_q2[l].astype(F32) * lam_k2[l].astype(F32))) + lam_init)
        s0 = jnp.zeros((n_prompt, RWKV_HEADS, RWKV_HEAD, RWKV_HEAD), F32)
        r0 = jnp.zeros((n_prompt, 1, RWKV_PROJ), F32)
        xp, kp, vp, sp, rp = _trunk_layer(xp, lp, w_bf, moe_w, lam, lam_init, s0, r0)
        xs, ks, vs, ss, rs = _trunk_layer(xs, lp, w_bf, moe_w, lam, lam_init, state_wkv[l], state_shift[l],
                                          cache=(cache_k[l], cache_v[l]))
        st_p.append((kp, vp, sp, rp))
        st_s.append((ks, vs, ss, rs))
    stack = lambda st, j: jnp.stack([t[j] for t in st])
    return (xp[:, N_META:], xs,
            stack(st_p, 0), stack(st_p, 1), stack(st_p, 2), stack(st_p, 3),
            stack(st_s, 0), stack(st_s, 1), stack(st_s, 2), stack(st_s, 3))
```

```python
import functools
import math

import jax
import jax.numpy as jnp
import numpy as np
from jax import lax
from jax.experimental import pallas as pl
from jax.experimental.pallas import tpu as pltpu

F32 = jnp.float32
BF16 = jnp.bfloat16

D_MODEL = 1024
CHUNK = 64
CHUNK_SHIFT = CHUNK.bit_length() - 1
assert 1 << CHUNK_SHIFT == CHUNK
N_META = 16
RWKV_HEAD = 64
RWKV_HEADS = D_MODEL // RWKV_HEAD
RWKV_DIM = RWKV_HEADS * RWKV_HEAD
DECAY_LORA = 64
AAA_LORA = 64
GATE_LORA = 128
RWKV_PROJ = 3 * RWKV_DIM + DECAY_LORA + AAA_LORA + GATE_LORA
LNX_EPS = 64e-5
ATT_HEADS = 8
QK_DIM = 64
V_DIM = 2 * QK_DIM
ATT_QK = ATT_HEADS * 2 * QK_DIM
ATT_DIM = ATT_HEADS * V_DIM
N_EXPERTS = 32
TOP_K = 4
SWIGLU_ALPHA = 1.702
SWIGLU_LIMIT = 7.0
RMS_EPS = 1e-6
NEG_INF = -1e30
LOG2E = math.log2(math.e)
Q_SCALE = QK_DIM ** -0.5 * LOG2E

LANES = 128
SUBLANES = 8
BF16_ROWS = 16
MXU_DIM = 256
VMEM_LIMIT = 56 * 1024 * 1024

GROUP_BLOCK = MXU_DIM
MOE_BLOCK = 256
ATT_TQ = 256
ATT_TK = 512
SCAN_TT = 24
SCAN_UNROLL = 8


def _cparams(*sem):
    return pltpu.CompilerParams(dimension_semantics=sem, vmem_limit_bytes=VMEM_LIMIT)


def _tile(n, target, mult):
    best = None
    for t in range(mult, min(n, target) + 1, mult):
        if n % t == 0:
            best = t
    return best if best is not None else n


def _group_ones():
    g = np.arange(GROUP_BLOCK) // RWKV_HEAD
    return jnp.asarray((g[:, None] == g[None, :]).astype(np.float32), BF16)


def _gsum(x, ones_ref):
    hi = x.astype(BF16)
    lo = (x - hi.astype(F32)).astype(BF16)
    ones = ones_ref[...]
    outs = []
    for c in range(x.shape[1] // GROUP_BLOCK):
        sl = slice(c * GROUP_BLOCK, (c + 1) * GROUP_BLOCK)
        outs.append(jnp.dot(hi[:, sl], ones, preferred_element_type=F32)
                    + jnp.dot(lo[:, sl], ones, preferred_element_type=F32))
    return jnp.concatenate(outs, axis=1)


def _norm_matmul_kernel(x_ref, g_ref, w_ref, o_ref):
    x = x_ref[...]
    h = x * lax.rsqrt(jnp.mean(x * x, axis=-1, keepdims=True) + RMS_EPS) * g_ref[...]
    o_ref[...] = jnp.dot(h.astype(BF16), w_ref[...], preferred_element_type=F32)


def _norm_matmul(x, g, w, tm):
    T, D = x.shape
    N = w.shape[1]
    return pl.pallas_call(
        _norm_matmul_kernel,
        grid=(T // tm,),
        in_specs=[pl.BlockSpec((tm, D), lambda i: (i, 0)),
                  pl.BlockSpec((1, D), lambda i: (0, 0)),
                  pl.BlockSpec((D, N), lambda i: (0, 0))],
        out_specs=pl.BlockSpec((tm, N), lambda i: (i, 0)),
        out_shape=jax.ShapeDtypeStruct((T, N), F32),
        compiler_params=_cparams("parallel"),
        name="norm_matmul",
    )(x, g.reshape(1, D), w)


def _rwkv_prep_kernel(pr_ref, prev_ref, mu_ref, w0_ref, wup_ref, a0_ref, aup_ref, gup_ref,
                      kk_ref, ka_ref, rk_ref, ones_ref,
                      kn_o, d_o, b_o, k_o, v_o, r_o, bonus_o, g_o):
    pr = pr_ref[...]
    row = lax.broadcasted_iota(jnp.int32, pr.shape, 0)
    prev = jnp.where(row == 0, prev_ref[...], pltpu.roll(pr, 1, axis=0))
    xs = pr + (prev - pr) * mu_ref[...]
    r = xs[:, :RWKV_DIM]
    k = xs[:, RWKV_DIM:2 * RWKV_DIM]
    v = xs[:, 2 * RWKV_DIM:3 * RWKV_DIM]
    lora_in = xs[:, 3 * RWKV_DIM:3 * RWKV_DIM + DECAY_LORA + AAA_LORA]
    gd = xs[:, 3 * RWKV_DIM + DECAY_LORA + AAA_LORA:]
    wl = jnp.dot(jnp.tanh(lora_in).astype(BF16), wup_ref[...], preferred_element_type=F32)
    al = jnp.dot(lora_in.astype(BF16), aup_ref[...], preferred_element_type=F32)
    z = -(w0_ref[...] + wl)
    w = -(jnp.maximum(z, 0.0) + jnp.log1p(jnp.exp(-jnp.abs(z)))) - 0.5
    decay = jnp.exp(-jnp.exp(w))
    a = jax.nn.sigmoid(a0_ref[...] + al)
    g = jnp.dot(jax.nn.sigmoid(gd).astype(BF16), gup_ref[...], preferred_element_type=F32)
    kk = k * kk_ref[...]
    nrm = jnp.sqrt(_gsum(kk * kk, ones_ref))
    kk = kk / jnp.maximum(nrm, 1e-12)
    k = k * (1.0 + (a - 1.0) * ka_ref[...])
    slab = lambda val: val.reshape(val.shape[0], RWKV_DIM // LANES, LANES)
    kn_o[...] = slab(-kk)
    d_o[...] = slab(decay)
    b_o[...] = slab(kk * a)
    k_o[...] = slab(k)
    v_o[...] = slab(v)
    r_o[...] = slab(r)
    bonus_o[...] = _gsum(r * k * rk_ref[...], ones_ref) * v
    g_o[...] = g


def _rwkv_prep(pr, prev_rows, lp, B, L, tl):
    nt = L // tl
    rows_per_token = RWKV_DIM // LANES
    row = lambda a: a.reshape(1, -1).astype(F32)
    zpad = jnp.zeros((DECAY_LORA, RWKV_DIM), F32)
    wup = jnp.concatenate([lp['w_up'], zpad], axis=0).astype(BF16)
    aup = jnp.concatenate([zpad, lp['a_up']], axis=0).astype(BF16)
    vec = lambda n: pl.BlockSpec((1, n), lambda b, i: (0, 0))
    mat = lambda m, n: pl.BlockSpec((m, n), lambda b, i: (0, 0))
    tok = pl.BlockSpec((tl, RWKV_DIM), lambda b, i: (b * nt + i, 0))
    time_major = pl.BlockSpec((tl, rows_per_token, LANES), lambda b, i: (i, b, 0))
    return pl.pallas_call(
        _rwkv_prep_kernel,
        grid=(B, nt),
        in_specs=[pl.BlockSpec((tl, RWKV_PROJ), lambda b, i: (b * nt + i, 0)),
                  pl.BlockSpec((None, 1, RWKV_PROJ), lambda b, i: (b * nt + i, 0, 0)),
                  vec(RWKV_PROJ), vec(RWKV_DIM), mat(2 * DECAY_LORA, RWKV_DIM), vec(RWKV_DIM),
                  mat(2 * AAA_LORA, RWKV_DIM), mat(GATE_LORA, RWKV_DIM),
                  vec(RWKV_DIM), vec(RWKV_DIM), vec(RWKV_DIM), mat(GROUP_BLOCK, GROUP_BLOCK)],
        out_specs=[time_major] * 6 + [tok] * 2,
        out_shape=[jax.ShapeDtypeStruct((L, B * rows_per_token, LANES), F32)] * 6
                  + [jax.ShapeDtypeStruct((B * L, RWKV_DIM), F32)] * 2,
        compiler_params=_cparams("parallel", "parallel"),
        name="rwkv_prep",
    )(pr, prev_rows, row(lp['shift_mu']), row(lp['w0']), wup, row(lp['a0']), aup,
      lp['g_up'].astype(BF16), row(lp['k_k']), row(lp['k_a']), row(lp['r_k']), _group_ones())


def _rwkv_scan_kernel(kn_ref, d_ref, b_ref, k_ref, v_ref, r_ref, s0_ref, y_ref, sfin_ref,
                      s_sc, opa_sc, opb_sc, yt_sc):
    tb = pl.program_id(1)

    @pl.when(tb == 0)
    def _():
        s_sc[...] = s0_ref[...]

    tt = kn_ref.shape[0]
    refs = (kn_ref, d_ref, b_ref, k_ref, v_ref, r_ref)

    def stage(t, op_sc):
        t = jnp.minimum(t, tt - 1)
        for i, ref in enumerate(refs):
            mt = ref[t].T
            op_sc[i] = jnp.concatenate([mt[:RWKV_HEAD], mt[RWKV_HEAD:]], axis=1)

    def step(t, op_sc):
        kn, d, bb, kk, rr = op_sc[0], op_sc[1], op_sc[2], op_sc[3], op_sc[5]

        def vstep(vi, c):
            sv = s_sc[vi]
            sa = jnp.sum(sv * kn, axis=0, keepdims=True)
            vv = op_sc[4, pl.ds(vi, 1), :]
            sn = sv * d + sa * bb + vv * kk
            s_sc[vi] = sn
            yt_sc[pl.ds(vi, 1), :] = jnp.sum(sn * rr, axis=0, keepdims=True)
            return c

        lax.fori_loop(0, RWKV_HEAD, vstep, 0, unroll=True)
        y = yt_sc[...]
        y_ref[t] = jnp.concatenate([y[:, :RWKV_HEAD], y[:, RWKV_HEAD:]], axis=0).T

    stage(0, opa_sc)

    def pair(i, carry):
        t0 = 2 * i
        stage(t0 + 1, opb_sc)
        step(t0, opa_sc)
        stage(t0 + 2, opa_sc)
        step(t0 + 1, opb_sc)
        return carry

    lax.fori_loop(0, tt // 2, pair, 0)

    @pl.when(tb == pl.num_programs(1) - 1)
    def _():
        sfin_ref[...] = s_sc[...]


def _scan_lane_order(nc):
    lane = np.arange(nc)
    return (lane // LANES) * LANES + 2 * (lane % RWKV_HEAD) + (lane % LANES) // RWKV_HEAD


def _rwkv_scan(ops, s0, tt):
    L, pairs, _ = ops[0].shape
    n = RWKV_HEAD
    nc = 2 * pairs
    assert tt % 2 == 0 and L % tt == 0 and nc % LANES == 0
    tspec = pl.BlockSpec((tt, n, LANES), lambda c, t: (t, c, 0))
    sspec = pl.BlockSpec((n, n, LANES), lambda c, t: (0, 0, c))
    return pl.pallas_call(
        _rwkv_scan_kernel,
        grid=(nc // LANES, L // tt),
        in_specs=[tspec] * 6 + [sspec],
        out_specs=[tspec, sspec],
        out_shape=[jax.ShapeDtypeStruct((L, pairs, LANES), F32), jax.ShapeDtypeStruct((n, n, nc), F32)],
        scratch_shapes=[pltpu.VMEM((n, n, LANES), F32), pltpu.VMEM((6, n, LANES), F32),
                        pltpu.VMEM((6, n, LANES), F32), pltpu.VMEM((n, LANES), F32)],
        compiler_params=_cparams("parallel", "arbitrary"),
        name="rwkv_scan",
    )(*ops, s0)


def _rwkv_post_kernel(y_ref, bonus_ref, g_ref, lg_ref, lb_ref, ones_ref, o_ref):
    y = jnp.concatenate([y_ref[:, j, :] for j in range(y_ref.shape[1])], axis=1)
    inv_n = 1.0 / RWKV_HEAD
    mean = _gsum(y, ones_ref) * inv_n
    yc = y - mean
    var = _gsum(yc * yc, ones_ref) * inv_n
    yn = yc * lax.rsqrt(var + LNX_EPS) * lg_ref[...] + lb_ref[...]
    o_ref[...] = ((yn + bonus_ref[...]) * g_ref[...]).astype(BF16)


def _rwkv_post(y, bonus, g, lp, B, L, tl):
    nt = L // tl
    tok = pl.BlockSpec((tl, RWKV_DIM), lambda b, i: (b * nt + i, 0))
    vec = pl.BlockSpec((1, RWKV_DIM), lambda b, i: (0, 0))
    return pl.pallas_call(
        _rwkv_post_kernel,
        grid=(B, nt),
        in_specs=[pl.BlockSpec((tl, RWKV_DIM // LANES, LANES), lambda b, i: (i, b, 0)), tok, tok, vec, vec,
                  pl.BlockSpec((GROUP_BLOCK, GROUP_BLOCK), lambda b, i: (0, 0))],
        out_specs=tok,
        out_shape=jax.ShapeDtypeStruct((B * L, RWKV_DIM), BF16),
        compiler_params=_cparams("parallel", "parallel"),
        name="rwkv_post",
    )(y, bonus, g, lp['lnx_g'].reshape(1, -1), lp['lnx_b'].reshape(1, -1), _group_ones())


def _qk_norm_kernel(qk_ref, v_ref, qg_ref, kg_ref, ones_ref, kout_ref, qb_ref, kb_ref, vb_ref, *, length):
    tl = qk_ref.shape[0]
    valid = (pl.program_id(1) * tl + lax.broadcasted_iota(jnp.int32, (tl, 1), 0)) < length
    q = qk_ref[:, :ATT_QK]
    k = qk_ref[:, ATT_QK:]
    inv_n = 1.0 / QK_DIM
    qn = q * lax.rsqrt(_gsum(q * q, ones_ref) * inv_n + RMS_EPS) * qg_ref[...]
    kn = k * lax.rsqrt(_gsum(k * k, ones_ref) * inv_n + RMS_EPS) * kg_ref[...]
    kout_ref[...] = kn
    qb_ref[...] = jnp.where(valid, qn * Q_SCALE, 0.0).astype(BF16)
    kb_ref[...] = jnp.where(valid, kn, 0.0).astype(BF16)
    vb_ref[...] = jnp.where(valid, v_ref[...], 0.0).astype(BF16)


def _qk_norm(pqk, pv, lp, lpad, tl):
    B, L, _ = pqk.shape
    tile_g = lambda g: jnp.tile(g.astype(F32), ATT_QK // QK_DIM).reshape(1, ATT_QK)
    blk = lambda n: pl.BlockSpec((None, tl, n), lambda b, i: (b, i, 0))
    vec = pl.BlockSpec((1, ATT_QK), lambda b, i: (0, 0))
    bshape = jax.ShapeDtypeStruct((B, lpad, ATT_QK), BF16)
    return pl.pallas_call(
        functools.partial(_qk_norm_kernel, length=L),
        grid=(B, lpad // tl),
        in_specs=[blk(2 * ATT_QK), blk(ATT_DIM), vec, vec,
                  pl.BlockSpec((GROUP_BLOCK, GROUP_BLOCK), lambda b, i: (0, 0))],
        out_specs=[blk(ATT_QK)] * 4,
        out_shape=[jax.ShapeDtypeStruct((B, L, ATT_QK), F32), bshape, bshape, bshape],
        compiler_params=_cparams("parallel", "parallel"),
        name="qk_norm",
    )(pqk, pv, tile_g(lp['q_norm_g']), tile_g(lp['k_norm_g']), _group_ones())


def _attn_kernel(slope_ref, lam_ref, q_ref, k_ref, v_ref, g_ref, o_ref,
                 qt_sc, vt_sc, rel_sc, sa_sc, sb_sc, m_sc, l_sc, a_sc,
                 *, chunked, q_pos0, n_keys, out_scale):
    tq = q_ref.shape[0]
    n_tiles, _, tk = vt_sc.shape
    h = pl.program_id(1)
    i = pl.program_id(2)

    @pl.when(i == 0)
    def _():
        def transpose_tile(j, carry):
            k0 = pl.multiple_of(j * tk, tk)
            vt_sc[j] = v_ref[pl.ds(k0, tk), :].astype(F32).T.astype(BF16)
            return carry
        lax.fori_loop(0, n_tiles, transpose_tile, 0)

    qt = q_ref[...].astype(F32).T
    sub = lax.broadcasted_iota(jnp.int32, qt.shape, 0)
    qt_sc[0] = jnp.where(sub < QK_DIM, qt, 0.0).astype(BF16)
    qt_sc[1] = jnp.where(sub >= QK_DIM, qt, 0.0).astype(BF16)
    q_first = q_pos0 + i * tq
    qpos = q_first + lax.broadcasted_iota(jnp.int32, (1, tq), 1)
    if chunked:
        chunk_end = lambda p: N_META + CHUNK + (((p - N_META) >> CHUNK_SHIFT) << CHUNK_SHIFT)
        vis = jnp.where(qpos < N_META, N_META, chunk_end(qpos))
        vis_last = chunk_end(q_first + tq - 1)
        n_kt = (jnp.minimum(vis_last, n_keys) + tk - 1) // tk
    else:
        vis = jnp.full((1, tq), n_keys, jnp.int32)
        n_kt = (n_keys + tk - 1) // tk
    n_past = jnp.minimum((q_first + 1) // tk, n_kt)
    slope = slope_ref[h] * LOG2E
    rel_sc[...] = slope * (lax.broadcasted_iota(jnp.int32, (tk, tq), 1)
                           - lax.broadcasted_iota(jnp.int32, (tk, tq), 0)).astype(F32)

    m_sc[...] = jnp.full(m_sc.shape, -jnp.inf, F32)
    l_sc[...] = jnp.zeros(l_sc.shape, F32)
    a_sc[...] = jnp.zeros(a_sc.shape, F32)

    def accumulate(mi, s, shift, vtj):
        m_old = m_sc[mi]
        m_new = jnp.maximum(m_old, jnp.max(s, axis=0, keepdims=True) + shift)
        alpha = jnp.exp2(m_old - m_new)
        p = jnp.exp2(s - (m_new - shift))
        l_sc[mi] = alpha * l_sc[mi] + jnp.sum(p, axis=0, keepdims=True)
        a_sc[mi] = alpha * a_sc[mi] + jnp.dot(vtj, p.astype(BF16), preferred_element_type=F32)
        m_sc[mi] = m_new

    def scores_into(s_ref, j):
        kj = k_ref[pl.ds(pl.multiple_of(jnp.minimum(j, n_tiles - 1) * tk, tk), tk), :]
        for mi in range(2):
            s_ref[mi] = jnp.dot(kj, qt_sc[mi], preferred_element_type=F32)

    def consume(s_ref, j, shift):
        vtj = vt_sc[j]
        for mi in range(2):
            accumulate(mi, s_ref[mi] - rel_sc[...], shift, vtj)

    past_shift = lambda j: -slope * (q_first - j * tk).astype(F32)

    n_pairs = n_past // 2

    @pl.when(n_pairs > 0)
    def _():
        scores_into(sa_sc, 0)

    def pair_body(jj, carry):
        j0 = 2 * jj
        scores_into(sb_sc, j0 + 1)
        consume(sa_sc, j0, past_shift(j0))
        scores_into(sa_sc, j0 + 2)
        consume(sb_sc, j0 + 1, past_shift(j0 + 1))
        return carry

    lax.fori_loop(0, n_pairs, pair_body, 0)

    def single_body(j, carry):
        @pl.when(j >= n_past)
        def _():
            kpos = j * tk + lax.broadcasted_iota(jnp.int32, (tk, 1), 0)
            bias = slope * jnp.abs(qpos - kpos).astype(F32)
            rel_sc[...] = jnp.where(kpos < vis, bias, -NEG_INF)

        scores_into(sa_sc, j)
        consume(sa_sc, j, jnp.where(j < n_past, past_shift(j), 0.0))
        return carry

    lax.fori_loop(2 * n_pairs, n_kt, single_body, 0)

    o = a_sc[0] / l_sc[0] - lam_ref[0] * (a_sc[1] / l_sc[1])
    o = o * lax.rsqrt(jnp.mean(o * o, axis=0, keepdims=True) + RMS_EPS) * g_ref[...] * out_scale
    o_ref[...] = o.T.astype(BF16)


def _attention(qb, kb, vb, lam, subln_g, lam_init, *, chunked, q_pos0, n_keys, n_q, tq, tk):
    B = qb.shape[0]
    lk = kb.shape[1]
    slopes = jnp.asarray([2.0 ** (-8.0 * (i + 1) / ATT_HEADS) for i in range(ATT_HEADS)], F32)
    gain = jnp.broadcast_to(subln_g.astype(F32).reshape(V_DIM, 1), (V_DIM, tq))
    smem = pl.BlockSpec(memory_space=pltpu.SMEM)
    qspec = pl.BlockSpec((None, tq, V_DIM), lambda b, h, i: (b, i, h))
    kspec = pl.BlockSpec((None, lk, V_DIM), lambda b, h, i: (b, 0, h))
    return pl.pallas_call(
        functools.partial(_attn_kernel, chunked=chunked, q_pos0=q_pos0, n_keys=n_keys,
                          out_scale=1.0 - lam_init),
        grid=(B, ATT_HEADS, n_q),
        in_specs=[smem, smem, qspec, kspec, kspec, pl.BlockSpec((V_DIM, tq), lambda b, h, i: (0, 0))],
        out_specs=qspec,
        out_shape=jax.ShapeDtypeStruct((B, n_q * tq, ATT_DIM), BF16),
        scratch_shapes=[pltpu.VMEM((2, V_DIM, tq), BF16), pltpu.VMEM((lk // tk, V_DIM, tk), BF16),
                        pltpu.VMEM((tk, tq), F32),
                        pltpu.VMEM((2, tk, tq), F32), pltpu.VMEM((2, tk, tq), F32),
                        pltpu.VMEM((2, 1, tq), F32), pltpu.VMEM((2, 1, tq), F32),
                        pltpu.VMEM((2, V_DIM, tq), F32)],
        compiler_params=_cparams("parallel", "parallel", "arbitrary"),
        name="diff_attention",
    )(slopes, lam.reshape(1).astype(F32), qb, kb, vb, gain)


def _merge_kernel(or_ref, oa_ref, pg_ref, x_ref, wr_ref, wa_ref, wo_ref, g2_ref, rw_ref, rb_ref,
                  x1_ref, h2_ref, lg_ref):
    gates = jax.nn.sigmoid(pg_ref[...])
    br = jnp.dot(or_ref[...], wr_ref[...], preferred_element_type=F32)
    ba = jnp.dot(oa_ref[...], wa_ref[...], preferred_element_type=F32)
    merged = gates[:, :D_MODEL] * br + gates[:, D_MODEL:] * ba
    x1 = x_ref[...] + jnp.dot(merged.astype(BF16), wo_ref[...], preferred_element_type=F32)
    h2 = x1 * lax.rsqrt(jnp.mean(x1 * x1, axis=-1, keepdims=True) + RMS_EPS) * g2_ref[...]
    x1_ref[...] = x1
    h2_ref[...] = h2.astype(BF16)
    lg_ref[...] = jnp.dot(h2, rw_ref[...], preferred_element_type=F32,
                          precision=lax.Precision.HIGHEST) + rb_ref[...]


def _merge(o_r, o_a, pg, x, lp, tm):
    T = x.shape[0]
    rw = jnp.zeros((D_MODEL, LANES), F32).at[:, :N_EXPERTS].set(lp['router_w'].astype(F32))
    rb = jnp.zeros((1, LANES), F32).at[0, :N_EXPERTS].set(lp['router_b'].astype(F32))
    tok = lambda n: pl.BlockSpec((tm, n), lambda i: (i, 0))
    mat = lambda m, n: pl.BlockSpec((m, n), lambda i: (0, 0))
    return pl.pallas_call(
        _merge_kernel,
        grid=(T // tm,),
        in_specs=[tok(RWKV_DIM), tok(ATT_DIM), tok(2 * D_MODEL), tok(D_MODEL),
                  mat(RWKV_DIM, D_MODEL), mat(ATT_DIM, D_MODEL), mat(D_MODEL, D_MODEL),
                  mat(1, D_MODEL), mat(D_MODEL, LANES), mat(1, LANES)],
        out_specs=[tok(D_MODEL), tok(D_MODEL), tok(LANES)],
        out_shape=[jax.ShapeDtypeStruct((T, D_MODEL), F32), jax.ShapeDtypeStruct((T, D_MODEL), BF16),
                   jax.ShapeDtypeStruct((T, LANES), F32)],
        compiler_params=_cparams("parallel"),
        name="merge_router",
    )(o_r, o_a, pg, x, lp['w_br_r'].astype(BF16), lp['w_br_a'].astype(BF16), lp['w_out'].astype(BF16),
      lp['norm2_g'].reshape(1, -1), rw, rb)


def _moe_kernel(be_ref, nact_ref, x_ref, gate_ref, wgu_ref, bgu_ref, wdn_ref, bdn_ref, o_ref,
                wgu_sc, wdn_sc):
    i = pl.program_id(0)
    active = i < nact_ref[0]
    new_expert = jnp.logical_or(i == 0, be_ref[i] != be_ref[jnp.maximum(i - 1, 0)])

    @pl.when(jnp.logical_and(active, new_expert))
    def _():
        for dst, src in ((wgu_sc, wgu_ref), (wdn_sc, wdn_ref)):
            for r0 in range(0, src.shape[0], MXU_DIM):
                dst[r0:r0 + MXU_DIM, :] = src[r0:r0 + MXU_DIM, :].astype(BF16)

    @pl.when(active)
    def _():
        gu = jnp.dot(x_ref[...], wgu_sc[...], preferred_element_type=F32) + bgu_ref[...]
        d_ff = gu.shape[1] // 2
        x_glu = jnp.minimum(gu[:, :d_ff], SWIGLU_LIMIT)
        x_lin = jnp.clip(gu[:, d_ff:], -SWIGLU_LIMIT, SWIGLU_LIMIT)
        act = x_glu * jax.nn.sigmoid(SWIGLU_ALPHA * x_glu) * (x_lin + 1.0)
        out = jnp.dot(act.astype(BF16), wdn_sc[...], preferred_element_type=F32) + bdn_ref[...]
        o_ref[...] = out * gate_ref[...]

    @pl.when(i >= nact_ref[0])
    def _():
        o_ref[...] = jnp.zeros_like(o_ref)


def _moe_ffn_blocks(xs, row_gate, blk_e, n_act, w_gu, b_gu, w_dn, b_dn):
    n_rows, D = xs.shape
    n_blk = n_rows // MOE_BLOCK
    d_ff2 = w_gu.shape[2]
    grid_spec = pltpu.PrefetchScalarGridSpec(
        num_scalar_prefetch=2,
        grid=(n_blk,),
        in_specs=[pl.BlockSpec((MOE_BLOCK, D), lambda i, be, na: (i, 0)),
                  pl.BlockSpec((MOE_BLOCK, 1), lambda i, be, na: (i, 0)),
                  pl.BlockSpec((None, D, d_ff2), lambda i, be, na: (be[i], 0, 0)),
                  pl.BlockSpec((None, 1, d_ff2), lambda i, be, na: (be[i], 0, 0)),
                  pl.BlockSpec((None, d_ff2 // 2, D), lambda i, be, na: (be[i], 0, 0)),
                  pl.BlockSpec((None, 1, D), lambda i, be, na: (be[i], 0, 0))],
        out_specs=pl.BlockSpec((MOE_BLOCK, D), lambda i, be, na: (i, 0)),
        scratch_shapes=[pltpu.VMEM((D, d_ff2), BF16), pltpu.VMEM((d_ff2 // 2, D), BF16)],
    )
    return pl.pallas_call(
        _moe_kernel,
        grid_spec=grid_spec,
        out_shape=jax.ShapeDtypeStruct((n_rows, D), F32),
        compiler_params=_cparams("arbitrary"),
        name="moe_ffn",
    )(blk_e, n_act, xs, row_gate.reshape(n_rows, 1), w_gu, b_gu.reshape(N_EXPERTS, 1, d_ff2),
      w_dn, b_dn.reshape(N_EXPERTS, 1, D))


def _moe(h2, logits, x1, moe_w):
    T, D = h2.shape
    top_v, top_e = lax.top_k(logits, TOP_K)
    gate = jax.nn.softmax(top_v, axis=-1)
    rows = T * TOP_K
    flat_e = top_e.reshape(-1).astype(jnp.int32)
    e_sorted, order = lax.sort((flat_e, jnp.arange(rows, dtype=jnp.int32)), num_keys=1, is_stable=True)
    rank = jnp.argsort(order).astype(jnp.int32)
    bounds = jnp.searchsorted(e_sorted, jnp.arange(N_EXPERTS + 1, dtype=jnp.int32)).astype(jnp.int32)
    start, counts = bounds[:-1], bounds[1:] - bounds[:-1]
    padded = (counts + MOE_BLOCK - 1) // MOE_BLOCK * MOE_BLOCK
    pad_end = jnp.cumsum(padded)
    pad_start = pad_end - padded
    n_blk = -(-(rows + N_EXPERTS * (MOE_BLOCK - 1)) // MOE_BLOCK)
    n_rows = n_blk * MOE_BLOCK
    blk_start = jnp.arange(n_blk, dtype=jnp.int32) * MOE_BLOCK
    blk_e = jnp.minimum(jnp.sum(blk_start[:, None] >= pad_end[None, :], axis=1, dtype=jnp.int32), N_EXPERTS - 1)
    n_act = (pad_end[-1:] // MOE_BLOCK).astype(jnp.int32)
    row_off = (blk_start - pad_start[blk_e])[:, None] + jnp.arange(MOE_BLOCK, dtype=jnp.int32)[None, :]
    row_valid = row_off < counts[blk_e][:, None]
    row_flat = order[jnp.where(row_valid, start[blk_e][:, None] + row_off, 0)]
    row_tok = jnp.where(row_valid, row_flat // TOP_K, T).reshape(n_rows)
    row_gate = jnp.where(row_valid, gate.reshape(-1)[row_flat], 0.0).reshape(n_rows)
    pos = (pad_start[flat_e] + rank - start[flat_e]).reshape(T, TOP_K)

    h_pad = jnp.concatenate([h2, jnp.zeros((1, D), h2.dtype)], axis=0)
    xs = h_pad[row_tok]
    out = _moe_ffn_blocks(xs, row_gate, blk_e, n_act, *moe_w)
    y = x1
    for k in range(TOP_K):
        y = y + out[pos[:, k]]
    return y


def _trunk_layer(x, lp, w_bf, moe_w, lam, lam_init, s0, prev_row, cache=None):
    B, L, D = x.shape
    T = B * L
    x2 = x.reshape(T, D)
    tl = _tile(L, 320, BF16_ROWS)
    tm = _tile(T, 1024, BF16_ROWS) if cache is not None else _tile(L, 1024, BF16_ROWS)
    tm_small = _tile(tm, 320, BF16_ROWS)
    w_r, w_qk, w_v, w_g = w_bf

    pr = _norm_matmul(x2, lp['norm1_g'], w_r, tm_small)
    pqk = _norm_matmul(x2, lp['norm1_g'], w_qk, tm)
    pv = _norm_matmul(x2, lp['norm1_g'], w_v, tm)
    pg = _norm_matmul(x2, lp['norm1_g'], w_g, tm)

    pr3 = pr.reshape(B, L, RWKV_PROJ)
    tile_last = pr3.reshape(B, L // tl, tl, RWKV_PROJ)[:, :-1, -1]
    prev_rows = jnp.concatenate([prev_row.astype(F32), tile_last], axis=1).reshape(T // tl, 1, RWKV_PROJ)
    *scan_ops, bonus, g = _rwkv_prep(pr, prev_rows, lp, B, L, tl)
    nc = B * RWKV_HEADS
    lane_chain = _scan_lane_order(nc)
    s0c = s0.astype(F32).transpose(2, 3, 0, 1).reshape(RWKV_HEAD, RWKV_HEAD, nc)[:, :, lane_chain]
    y, s_fin_c = _rwkv_scan(scan_ops, s0c, _tile(L, SCAN_TT, 2))
    s_fin_c = s_fin_c[:, :, np.argsort(lane_chain)]
    s_fin = s_fin_c.reshape(RWKV_HEAD, RWKV_HEAD, B, RWKV_HEADS).transpose(2, 3, 0, 1)
    o_r = _rwkv_post(y, bonus, g, lp, B, L, tl)

    if cache is None:
        assert ATT_TK % ATT_TQ == 0
        lpad = -(-L // ATT_TK) * ATT_TK
        k_rows, qb, kb, vb = _qk_norm(pqk.reshape(B, L, 2 * ATT_QK), pv.reshape(B, L, ATT_DIM), lp, lpad,
                                      _tile(lpad, 512, BF16_ROWS))
        o_a = _attention(qb, kb, vb, lam, lp['subln_g'], lam_init, chunked=True, q_pos0=0, n_keys=L,
                         n_q=-(-L // ATT_TQ), tq=ATT_TQ, tk=ATT_TK)
    else:
        cache_k, cache_v = cache
        past = cache_k.shape[1]
        k_rows, qb, kb, vb = _qk_norm(pqk.reshape(B, L, 2 * ATT_QK), pv.reshape(B, L, ATT_DIM), lp, L, L)
        n_keys = past + L
        lk_pad = -(-n_keys // ATT_TK) * ATT_TK
        zk = jnp.zeros((B, lk_pad - n_keys, ATT_QK), BF16)
        k_all = jnp.concatenate([cache_k.reshape(B, past, ATT_QK).astype(BF16), kb, zk], axis=1)
        v_all = jnp.concatenate([cache_v.reshape(B, past, ATT_DIM).astype(BF16), vb, zk], axis=1)
        tq = -(-L // LANES) * LANES
        qb = jnp.concatenate([qb, jnp.zeros((B, tq - L, ATT_QK), BF16)], axis=1)
        o_a = _attention(qb, k_all, v_all, lam, lp['subln_g'], lam_init, chunked=False, q_pos0=past,
                         n_keys=n_keys, n_q=1, tq=tq, tk=ATT_TK)
    o_a = o_a[:, :L].reshape(T, ATT_DIM)

    x1, h2, logits = _merge(o_r, o_a, pg, x2, lp, tm_small)
    x_out = _moe(h2, logits[:, :N_EXPERTS], x1, moe_w)
    return (x_out.reshape(B, L, D), k_rows.reshape(B, L, ATT_HEADS, 2, QK_DIM),
            pv.reshape(B, L, ATT_HEADS, V_DIM), s_fin, pr3[:, -1:])


def kernel(x_prompt, x_sample, cache_k, cache_v, state_wkv, state_shift, meta, norm1_g, w_in, shift_mu, w0, w_up, a0, a_up, g_up, k_k, k_a, r_k, lnx_g, lnx_b, q_norm_g, k_norm_g, lam_q1, lam_k1, lam_q2, lam_k2, subln_g, w_br_r, w_br_a, w_out, norm2_g, router_w, router_b, w_gu, b_gu, w_dn, b_dn):
    depth = w_in.shape[0]
    n_prompt = x_prompt.shape[0]
    meta_rows = jnp.broadcast_to(meta.astype(x_prompt.dtype)[None], (n_prompt, N_META, D_MODEL))
    xp = jnp.concatenate([meta_rows, x_prompt], axis=1)
    xs = x_sample
    st_p, st_s = [], []
    for l in range(depth):
        lp = dict(norm1_g=norm1_g[l], shift_mu=shift_mu[l], w0=w0[l], w_up=w_up[l],
                  a0=a0[l], a_up=a_up[l], g_up=g_up[l], k_k=k_k[l], k_a=k_a[l], r_k=r_k[l].reshape(-1),
                  lnx_g=lnx_g[l], lnx_b=lnx_b[l], q_norm_g=q_norm_g[l], k_norm_g=k_norm_g[l],
                  subln_g=subln_g[l], w_br_r=w_br_r[l], w_br_a=w_br_a[l], w_out=w_out[l],
                  norm2_g=norm2_g[l], router_w=router_w[l], router_b=router_b[l])
        wl = w_in[l].astype(BF16)
        c0, c1, c2 = RWKV_PROJ, RWKV_PROJ + 2 * ATT_QK, RWKV_PROJ + 2 * ATT_QK + ATT_DIM
        w_bf = (wl[:, :c0], wl[:, c0:c1], wl[:, c1:c2], wl[:, c2:])
        moe_w = (w_gu[l].astype(F32), b_gu[l].astype(F32), w_dn[l].astype(F32), b_dn[l].astype(F32))
        lam_init = 0.8 - 0.6 * math.exp(-0.3 * l)
        lam = (jnp.exp(jnp.sum(lam_q1[l].astype(F32) * lam_k1[l].astype(F32)))
               - jnp.exp(jnp.sum(lam_q2[l].astype(F32) * lam_k2[l].astype(F32))) + lam_init)
        s0 = jnp.zeros((n_prompt, RWKV_HEADS, RWKV_HEAD, RWKV_HEAD), F32)
        r0 = jnp.zeros((n_prompt, 1, RWKV_PROJ), F32)
        xp, kp, vp, sp, rp = _trunk_layer(xp, lp, w_bf, moe_w, lam, lam_init, s0, r0)
        xs, ks, vs, ss, rs = _trunk_layer(xs, lp, w_bf, moe_w, lam, lam_init, state_wkv[l], state_shift[l],
                                          cache=(cache_k[l], cache_v[l]))
        st_p.append((kp, vp, sp, rp))
        st_s.append((ks, vs, ss, rs))
    stack = lambda st, j: jnp.stack([t[j] for t in st])
    return (xp[:, N_META:], xs,
            stack(st_p, 0), stack(st_p, 1), stack(st_p, 2), stack(st_p, 3),
            stack(st_s, 0), stack(st_s, 1), stack(st_s, 2), stack(st_s, 3))
```

```python
import functools
import math

import jax
import jax.numpy as jnp
import numpy as np
from jax import lax
from jax.experimental import pallas as pl
from jax.experimental.pallas import tpu as pltpu

F32 = jnp.float32
BF16 = jnp.bfloat16

D_MODEL = 1024
CHUNK = 64
CHUNK_SHIFT = CHUNK.bit_length() - 1
assert 1 << CHUNK_SHIFT == CHUNK
N_META = 16
RWKV_HEAD = 64
RWKV_HEADS = D_MODEL // RWKV_HEAD
RWKV_DIM = RWKV_HEADS * RWKV_HEAD
DECAY_LORA = 64
AAA_LORA = 64
GATE_LORA = 128
RWKV_PROJ = 3 * RWKV_DIM + DECAY_LORA + AAA_LORA + GATE_LORA
LNX_EPS = 64e-5
ATT_HEADS = 8
QK_DIM = 64
V_DIM = 2 * QK_DIM
ATT_QK = ATT_HEADS * 2 * QK_DIM
ATT_DIM = ATT_HEADS * V_DIM
N_EXPERTS = 32
TOP_K = 4
SWIGLU_ALPHA = 1.702
SWIGLU_LIMIT = 7.0
RMS_EPS = 1e-6
NEG_INF = -1e30
LOG2E = math.log2(math.e)
Q_SCALE = QK_DIM ** -0.5 * LOG2E


def _bf16_terms(x, n):
    terms = []
    for _ in range(n):
        t = float(np.float32(x).astype(jnp.bfloat16))
        terms.append(t)
        x -= t
    return tuple(terms)


LOG2E_BF16_TERMS = _bf16_terms(LOG2E, 3)

LANES = 128
SUBLANES = 8
BF16_ROWS = 16
BF16_EXACT_INT = 256
MXU_DIM = 256
VMEM_LIMIT = 56 * 1024 * 1024

GROUP_BLOCK = MXU_DIM
MOE_BLOCK = 256
ATT_TQ = 256
ATT_TK = 512
SCAN_TT = 24
SCAN_ACCUMULATORS = 4


def _cparams(*sem):
    return pltpu.CompilerParams(dimension_semantics=sem, vmem_limit_bytes=VMEM_LIMIT)


def _tile(n, target, mult):
    best = None
    for t in range(mult, min(n, target) + 1, mult):
        if n % t == 0:
            best = t
    return best if best is not None else n


def _group_ones():
    g = np.arange(GROUP_BLOCK) // RWKV_HEAD
    return jnp.asarray((g[:, None] == g[None, :]).astype(np.float32), BF16)


def _gsum(x, ones_ref):
    hi = x.astype(BF16)
    lo = (x - hi.astype(F32)).astype(BF16)
    ones = ones_ref[...]
    outs = []
    for c in range(x.shape[1] // GROUP_BLOCK):
        sl = slice(c * GROUP_BLOCK, (c + 1) * GROUP_BLOCK)
        outs.append(jnp.dot(hi[:, sl], ones, preferred_element_type=F32)
                    + jnp.dot(lo[:, sl], ones, preferred_element_type=F32))
    return jnp.concatenate(outs, axis=1)


def _norm_matmul_kernel(x_ref, g_ref, w_ref, o_ref):
    x = x_ref[...]
    h = x * lax.rsqrt(jnp.mean(x * x, axis=-1, keepdims=True) + RMS_EPS) * g_ref[...]
    o_ref[...] = jnp.dot(h.astype(BF16), w_ref[...], preferred_element_type=F32)


def _norm_matmul(x, g, w, tm):
    T, D = x.shape
    N = w.shape[1]
    return pl.pallas_call(
        _norm_matmul_kernel,
        grid=(T // tm,),
        in_specs=[pl.BlockSpec((tm, D), lambda i: (i, 0)),
                  pl.BlockSpec((1, D), lambda i: (0, 0)),
                  pl.BlockSpec((D, N), lambda i: (0, 0))],
        out_specs=pl.BlockSpec((tm, N), lambda i: (i, 0)),
        out_shape=jax.ShapeDtypeStruct((T, N), F32),
        compiler_params=_cparams("parallel"),
        name="norm_matmul",
    )(x, g.reshape(1, D), w)


def _rwkv_prep_kernel(pr_ref, prev_ref, mu_ref, w0_ref, wup_ref, a0_ref, aup_ref, gup_ref,
                      kk_ref, ka_ref, rk_ref, ones_ref,
                      kn_o, d_o, b_o, k_o, v_o, r_o, bonus_o, g_o):
    pr = pr_ref[...]
    row = lax.broadcasted_iota(jnp.int32, pr.shape, 0)
    prev = jnp.where(row == 0, prev_ref[...], pltpu.roll(pr, 1, axis=0))
    xs = pr + (prev - pr) * mu_ref[...]
    r = xs[:, :RWKV_DIM]
    k = xs[:, RWKV_DIM:2 * RWKV_DIM]
    v = xs[:, 2 * RWKV_DIM:3 * RWKV_DIM]
    lora_in = xs[:, 3 * RWKV_DIM:3 * RWKV_DIM + DECAY_LORA + AAA_LORA]
    gd = xs[:, 3 * RWKV_DIM + DECAY_LORA + AAA_LORA:]
    wl = jnp.dot(jnp.tanh(lora_in).astype(BF16), wup_ref[...], preferred_element_type=F32)
    al = jnp.dot(lora_in.astype(BF16), aup_ref[...], preferred_element_type=F32)
    z = -(w0_ref[...] + wl)
    w = -(jnp.maximum(z, 0.0) + jnp.log1p(jnp.exp(-jnp.abs(z)))) - 0.5
    decay = jnp.exp(-jnp.exp(w))
    a = jax.nn.sigmoid(a0_ref[...] + al)
    g = jnp.dot(jax.nn.sigmoid(gd).astype(BF16), gup_ref[...], preferred_element_type=F32)
    kk = k * kk_ref[...]
    nrm = jnp.sqrt(_gsum(kk * kk, ones_ref))
    kk = kk / jnp.maximum(nrm, 1e-12)
    k = k * (1.0 + (a - 1.0) * ka_ref[...])
    slab = lambda val: val.reshape(val.shape[0], RWKV_DIM // LANES, LANES)
    kn_o[...] = slab(-kk)
    d_o[...] = slab(decay)
    b_o[...] = slab(kk * a)
    k_o[...] = slab(k)
    v_o[...] = slab(v)
    r_o[...] = slab(r)
    bonus_o[...] = _gsum(r * k * rk_ref[...], ones_ref) * v
    g_o[...] = g


def _rwkv_prep(pr, prev_rows, lp, B, L, tl):
    nt = L // tl
    rows_per_token = RWKV_DIM // LANES
    row = lambda a: a.reshape(1, -1).astype(F32)
    zpad = jnp.zeros((DECAY_LORA, RWKV_DIM), F32)
    wup = jnp.concatenate([lp['w_up'], zpad], axis=0).astype(BF16)
    aup = jnp.concatenate([zpad, lp['a_up']], axis=0).astype(BF16)
    vec = lambda n: pl.BlockSpec((1, n), lambda b, i: (0, 0))
    mat = lambda m, n: pl.BlockSpec((m, n), lambda b, i: (0, 0))
    tok = pl.BlockSpec((tl, RWKV_DIM), lambda b, i: (b * nt + i, 0))
    time_major = pl.BlockSpec((tl, rows_per_token, LANES), lambda b, i: (i, b, 0))
    return pl.pallas_call(
        _rwkv_prep_kernel,
        grid=(B, nt),
        in_specs=[pl.BlockSpec((tl, RWKV_PROJ), lambda b, i: (b * nt + i, 0)),
                  pl.BlockSpec((None, 1, RWKV_PROJ), lambda b, i: (b * nt + i, 0, 0)),
                  vec(RWKV_PROJ), vec(RWKV_DIM), mat(2 * DECAY_LORA, RWKV_DIM), vec(RWKV_DIM),
                  mat(2 * AAA_LORA, RWKV_DIM), mat(GATE_LORA, RWKV_DIM),
                  vec(RWKV_DIM), vec(RWKV_DIM), vec(RWKV_DIM), mat(GROUP_BLOCK, GROUP_BLOCK)],
        out_specs=[time_major] * 6 + [tok] * 2,
        out_shape=[jax.ShapeDtypeStruct((L, B * rows_per_token, LANES), F32)] * 6
                  + [jax.ShapeDtypeStruct((B * L, RWKV_DIM), F32)] * 2,
        compiler_params=_cparams("parallel", "parallel"),
        name="rwkv_prep",
    )(pr, prev_rows, row(lp['shift_mu']), row(lp['w0']), wup, row(lp['a0']), aup,
      lp['g_up'].astype(BF16), row(lp['k_k']), row(lp['k_a']), row(lp['r_k']), _group_ones())


def _rwkv_scan_kernel(kn_ref, d_ref, b_ref, k_ref, v_ref, r_ref, s0_ref, y_ref, sfin_ref,
                      s_sc, opa_sc, opb_sc):
    tb = pl.program_id(1)

    @pl.when(tb == 0)
    def _():
        s_sc[...] = s0_ref[...]

    tt = kn_ref.shape[0]
    refs = (kn_ref, d_ref, b_ref, k_ref, v_ref, r_ref)

    def stage(t, op_sc):
        t = jnp.minimum(t, tt - 1)
        for i, ref in enumerate(refs):
            mt = ref[t].T
            op_sc[i] = jnp.concatenate([mt[:RWKV_HEAD], mt[RWKV_HEAD:]], axis=1)

    def tree_sum(terms):
        while len(terms) > 1:
            terms = [a + b for a, b in zip(terms[::2], terms[1::2])]
        return terms[0]

    class Sums:
        def __init__(self):
            self.partial = [None] * SCAN_ACCUMULATORS

        def add(self, k, term):
            a = k % SCAN_ACCUMULATORS
            self.partial[a] = term if self.partial[a] is None else self.partial[a] + term

        def total(self):
            return tree_sum(self.partial)

    def state_dot(op_sc):
        acc = Sums()
        for k in range(RWKV_HEAD):
            acc.add(k, s_sc[k] * op_sc[0, k:k + 1, :])
        return acc.total()

    def step(t, op_sc, next_sc, sa):
        row = lambda i, k: op_sc[i, k:k + 1, :]
        vv = op_sc[4]
        y_acc, sa_acc = Sums(), Sums()
        for k in range(RWKV_HEAD):
            sn = s_sc[k] * row(1, k) + sa * row(2, k) + vv * row(3, k)
            s_sc[k] = sn
            y_acc.add(k, sn * row(5, k))
            sa_acc.add(k, sn * next_sc[0, k:k + 1, :])
        y = y_acc.total()
        y_ref[t] = jnp.concatenate([y[:, :RWKV_HEAD], y[:, RWKV_HEAD:]], axis=0).T
        return sa_acc.total()

    stage(0, opa_sc)

    def pair(i, sa):
        t0 = 2 * i
        stage(t0 + 1, opb_sc)
        sa = step(t0, opa_sc, opb_sc, sa)
        stage(t0 + 2, opa_sc)
        return step(t0 + 1, opb_sc, opa_sc, sa)

    lax.fori_loop(0, tt // 2, pair, state_dot(opa_sc))

    @pl.when(tb == pl.num_programs(1) - 1)
    def _():
        sfin_ref[...] = s_sc[...]


def _scan_lane_order(nc):
    lane = np.arange(nc)
    return (lane // LANES) * LANES + 2 * (lane % RWKV_HEAD) + (lane % LANES) // RWKV_HEAD


def _rwkv_scan(ops, s0, tt):
    L, pairs, _ = ops[0].shape
    n = RWKV_HEAD
    nc = 2 * pairs
    assert tt % 2 == 0 and L % tt == 0 and nc % LANES == 0
    tspec = pl.BlockSpec((tt, n, LANES), lambda c, t: (t, c, 0))
    sspec = pl.BlockSpec((n, n, LANES), lambda c, t: (0, 0, c))
    return pl.pallas_call(
        _rwkv_scan_kernel,
        grid=(nc // LANES, L // tt),
        in_specs=[tspec] * 6 + [sspec],
        out_specs=[tspec, sspec],
        out_shape=[jax.ShapeDtypeStruct((L, pairs, LANES), F32), jax.ShapeDtypeStruct((n, n, nc), F32)],
        scratch_shapes=[pltpu.VMEM((n, n, LANES), F32), pltpu.VMEM((6, n, LANES), F32),
                        pltpu.VMEM((6, n, LANES), F32)],
        compiler_params=_cparams("parallel", "arbitrary"),
        name="rwkv_scan",
    )(*ops, s0)


def _rwkv_post_kernel(y_ref, bonus_ref, g_ref, lg_ref, lb_ref, ones_ref, o_ref):
    y = jnp.concatenate([y_ref[:, j, :] for j in range(y_ref.shape[1])], axis=1)
    inv_n = 1.0 / RWKV_HEAD
    mean = _gsum(y, ones_ref) * inv_n
    yc = y - mean
    var = _gsum(yc * yc, ones_ref) * inv_n
    yn = yc * lax.rsqrt(var + LNX_EPS) * lg_ref[...] + lb_ref[...]
    o_ref[...] = ((yn + bonus_ref[...]) * g_ref[...]).astype(BF16)


def _rwkv_post(y, bonus, g, lp, B, L, tl):
    nt = L // tl
    tok = pl.BlockSpec((tl, RWKV_DIM), lambda b, i: (b * nt + i, 0))
    vec = pl.BlockSpec((1, RWKV_DIM), lambda b, i: (0, 0))
    return pl.pallas_call(
        _rwkv_post_kernel,
        grid=(B, nt),
        in_specs=[pl.BlockSpec((tl, RWKV_DIM // LANES, LANES), lambda b, i: (i, b, 0)), tok, tok, vec, vec,
                  pl.BlockSpec((GROUP_BLOCK, GROUP_BLOCK), lambda b, i: (0, 0))],
        out_specs=tok,
        out_shape=jax.ShapeDtypeStruct((B * L, RWKV_DIM), BF16),
        compiler_params=_cparams("parallel", "parallel"),
        name="rwkv_post",
    )(y, bonus, g, lp['lnx_g'].reshape(1, -1), lp['lnx_b'].reshape(1, -1), _group_ones())


def _qk_norm_kernel(qk_ref, v_ref, qg_ref, kg_ref, ones_ref, kout_ref, qb_ref, kb_ref, vb_ref, *, length):
    tl = qk_ref.shape[0]
    valid = (pl.program_id(1) * tl + lax.broadcasted_iota(jnp.int32, (tl, 1), 0)) < length
    q = qk_ref[:, :ATT_QK]
    k = qk_ref[:, ATT_QK:]
    inv_n = 1.0 / QK_DIM
    qn = q * lax.rsqrt(_gsum(q * q, ones_ref) * inv_n + RMS_EPS) * qg_ref[...]
    kn = k * lax.rsqrt(_gsum(k * k, ones_ref) * inv_n + RMS_EPS) * kg_ref[...]
    kout_ref[...] = kn
    qb_ref[...] = jnp.where(valid, qn * Q_SCALE, 0.0).astype(BF16)
    kb_ref[...] = jnp.where(valid, kn, 0.0).astype(BF16)
    vb_ref[...] = jnp.where(valid, v_ref[...], 0.0).astype(BF16)


def _qk_norm(pqk, pv, lp, lpad, tl):
    B, L, _ = pqk.shape
    tile_g = lambda g: jnp.tile(g.astype(F32), ATT_QK // QK_DIM).reshape(1, ATT_QK)
    blk = lambda n: pl.BlockSpec((None, tl, n), lambda b, i: (b, i, 0))
    vec = pl.BlockSpec((1, ATT_QK), lambda b, i: (0, 0))
    bshape = jax.ShapeDtypeStruct((B, lpad, ATT_QK), BF16)
    return pl.pallas_call(
        functools.partial(_qk_norm_kernel, length=L),
        grid=(B, lpad // tl),
        in_specs=[blk(2 * ATT_QK), blk(ATT_DIM), vec, vec,
                  pl.BlockSpec((GROUP_BLOCK, GROUP_BLOCK), lambda b, i: (0, 0))],
        out_specs=[blk(ATT_QK)] * 4,
        out_shape=[jax.ShapeDtypeStruct((B, L, ATT_QK), F32), bshape, bshape, bshape],
        compiler_params=_cparams("parallel", "parallel"),
        name="qk_norm",
    )(pqk, pv, tile_g(lp['q_norm_g']), tile_g(lp['k_norm_g']), _group_ones())


def _attn_kernel(slope_ref, lam_ref, q_ref, k_ref, kpos_ref, v_ref, g_ref, o_ref,
                 qt_sc, vt_sc, pen_sc, sa_sc, sb_sc, m_sc, a_sc,
                 *, chunked, q_pos0, n_keys, out_scale):
    tq = q_ref.shape[0]
    n_tiles, _, tk = vt_sc.shape
    h = pl.program_id(1)
    i = pl.program_id(2)

    @pl.when(i == 0)
    def _():
        ones_rows = (lax.broadcasted_iota(jnp.int32, (BF16_ROWS, tk), 0) == 0).astype(F32).astype(BF16)

        def transpose_tile(j, carry):
            k0 = pl.multiple_of(j * tk, tk)
            vt_sc[j, :V_DIM, :] = v_ref[pl.ds(k0, tk), :].astype(F32).T.astype(BF16)
            vt_sc[j, V_DIM:, :] = ones_rows
            return carry
        lax.fori_loop(0, n_tiles, transpose_tile, 0)

    qt = q_ref[...].astype(F32).T
    sub = lax.broadcasted_iota(jnp.int32, qt.shape, 0)
    slope = slope_ref[h]
    coef = jnp.zeros(qt.shape, F32)
    for r, c in enumerate([w * c for w in (CHUNK, 1) for c in LOG2E_BF16_TERMS]):
        coef = jnp.where(sub == r, slope * c, coef)
    coef = coef.astype(BF16)
    for mi, keep in enumerate((sub < QK_DIM, sub >= QK_DIM)):
        qt_sc[mi, :2 * QK_DIM, :] = jnp.where(keep, qt, 0.0).astype(BF16)
        qt_sc[mi, 2 * QK_DIM:, :] = coef
    q_first = q_pos0 + i * tq
    qpos = q_first + lax.broadcasted_iota(jnp.int32, (1, tq), 1)
    if chunked:
        chunk_end = lambda p: N_META + CHUNK + (((p - N_META) >> CHUNK_SHIFT) << CHUNK_SHIFT)
        vis = jnp.where(qpos < N_META, N_META, chunk_end(qpos))
        vis_last = chunk_end(q_first + tq - 1)
        n_kt = (jnp.minimum(vis_last, n_keys) + tk - 1) // tk
    else:
        vis = jnp.full((1, tq), n_keys, jnp.int32)
        n_kt = (n_keys + tk - 1) // tk
    n_past = jnp.minimum((q_first + 1) // tk, n_kt)

    m_sc[...] = jnp.full(m_sc.shape, -jnp.inf, F32)
    a_sc[...] = jnp.zeros(a_sc.shape, F32)

    def accumulate(mi, s, vtj):
        m_old = m_sc[mi]
        m_new = jnp.maximum(m_old, jnp.max(s, axis=0, keepdims=True))
        p = jnp.exp2(s - m_new)
        a_sc[mi] = jnp.exp2(m_old - m_new) * a_sc[mi] + jnp.dot(vtj, p.astype(BF16),
                                                                preferred_element_type=F32)
        m_sc[mi] = m_new

    def scores_into(s_ref, j):
        k0 = pl.multiple_of(jnp.minimum(j, n_tiles - 1) * tk, tk)
        kj = jnp.concatenate([k_ref[pl.ds(k0, tk), :], kpos_ref[pl.ds(k0, tk), :]], axis=1)
        for mi in range(2):
            s_ref[mi] = jnp.dot(kj, qt_sc[mi], preferred_element_type=F32)

    def consume(s_ref, j):
        vtj = vt_sc[j]
        for mi in range(2):
            accumulate(mi, s_ref[mi], vtj)

    n_pairs = n_past // 2

    @pl.when(n_pairs > 0)
    def _():
        scores_into(sa_sc, 0)

    def pair_body(jj, carry):
        j0 = 2 * jj
        scores_into(sb_sc, j0 + 1)
        consume(sa_sc, j0)
        scores_into(sa_sc, j0 + 2)
        consume(sb_sc, j0 + 1)
        return carry

    lax.fori_loop(0, n_pairs, pair_body, 0)

    def past_single(j, carry):
        scores_into(sa_sc, j)
        consume(sa_sc, j)
        return carry

    lax.fori_loop(2 * n_pairs, n_past, past_single, 0)

    def edge_body(j, carry):
        kpos = j * tk + lax.broadcasted_iota(jnp.int32, (tk, 1), 0)
        ahead = jnp.maximum(kpos - qpos, 0).astype(F32)
        pen_sc[...] = jnp.where(kpos < vis, (2.0 * LOG2E * slope) * ahead, -NEG_INF)
        scores_into(sa_sc, j)
        vtj = vt_sc[j]
        for mi in range(2):
            accumulate(mi, sa_sc[mi] - pen_sc[...], vtj)
        return carry

    lax.fori_loop(n_past, n_kt, edge_body, 0)

    num = [a_sc[mi, :V_DIM, :] / a_sc[mi, V_DIM:V_DIM + 1, :] for mi in range(2)]
    o = num[0] - lam_ref[0] * num[1]
    o = o * lax.rsqrt(jnp.mean(o * o, axis=0, keepdims=True) + RMS_EPS) * g_ref[...] * out_scale
    o_ref[...] = o.T.astype(BF16)


def _attention(qb, kb, vb, lam, subln_g, lam_init, *, chunked, q_pos0, n_keys, n_q, tq, tk):
    B = qb.shape[0]
    lk = kb.shape[1]
    slopes = jnp.asarray([2.0 ** (-8.0 * (i + 1) / ATT_HEADS) for i in range(ATT_HEADS)], F32)
    gain = jnp.broadcast_to(subln_g.astype(F32).reshape(V_DIM, 1), (V_DIM, tq))
    assert lk // CHUNK <= BF16_EXACT_INT
    kpos = np.arange(lk)
    digits = np.zeros((lk, V_DIM), np.float32)
    digits[:, 0:3] = (kpos // CHUNK)[:, None]
    digits[:, 3:6] = (kpos % CHUNK)[:, None]
    kpos_digits = jnp.asarray(digits, BF16)
    smem = pl.BlockSpec(memory_space=pltpu.SMEM)
    qspec = pl.BlockSpec((None, tq, V_DIM), lambda b, h, i: (b, i, h))
    kspec = pl.BlockSpec((None, lk, V_DIM), lambda b, h, i: (b, 0, h))
    const = lambda shape: pl.BlockSpec(shape, lambda b, h, i: (0, 0))
    acc_rows = V_DIM + BF16_ROWS
    return pl.pallas_call(
        functools.partial(_attn_kernel, chunked=chunked, q_pos0=q_pos0, n_keys=n_keys,
                          out_scale=1.0 - lam_init),
        grid=(B, ATT_HEADS, n_q),
        in_specs=[smem, smem, qspec, kspec, const((lk, V_DIM)), kspec, const((V_DIM, tq))],
        out_specs=qspec,
        out_shape=jax.ShapeDtypeStruct((B, n_q * tq, ATT_DIM), BF16),
        scratch_shapes=[pltpu.VMEM((2, 2 * V_DIM, tq), BF16), pltpu.VMEM((lk // tk, acc_rows, tk), BF16),
                        pltpu.VMEM((tk, tq), F32),
                        pltpu.VMEM((2, tk, tq), F32), pltpu.VMEM((2, tk, tq), F32),
                        pltpu.VMEM((2, 1, tq), F32),
                        pltpu.VMEM((2, acc_rows, tq), F32)],
        compiler_params=_cparams("parallel", "parallel", "arbitrary"),
        name="diff_attention",
    )(slopes, lam.reshape(1).astype(F32), qb, kb, kpos_digits, vb, gain)


def _merge_kernel(or_ref, oa_ref, pg_ref, x_ref, wr_ref, wa_ref, wo_ref, g2_ref, rw_ref, rb_ref,
                  x1_ref, h2_ref, lg_ref):
    gates = jax.nn.sigmoid(pg_ref[...])
    br = jnp.dot(or_ref[...], wr_ref[...], preferred_element_type=F32)
    ba = jnp.dot(oa_ref[...], wa_ref[...], preferred_element_type=F32)
    merged = gates[:, :D_MODEL] * br + gates[:, D_MODEL:] * ba
    x1 = x_ref[...] + jnp.dot(merged.astype(BF16), wo_ref[...], preferred_element_type=F32)
    h2 = x1 * lax.rsqrt(jnp.mean(x1 * x1, axis=-1, keepdims=True) + RMS_EPS) * g2_ref[...]
    x1_ref[...] = x1
    h2_ref[...] = h2.astype(BF16)
    lg_ref[...] = jnp.dot(h2, rw_ref[...], preferred_element_type=F32,
                          precision=lax.Precision.HIGHEST) + rb_ref[...]


def _merge(o_r, o_a, pg, x, lp, tm):
    T = x.shape[0]
    rw = jnp.zeros((D_MODEL, LANES), F32).at[:, :N_EXPERTS].set(lp['router_w'].astype(F32))
    rb = jnp.zeros((1, LANES), F32).at[0, :N_EXPERTS].set(lp['router_b'].astype(F32))
    tok = lambda n: pl.BlockSpec((tm, n), lambda i: (i, 0))
    mat = lambda m, n: pl.BlockSpec((m, n), lambda i: (0, 0))
    return pl.pallas_call(
        _merge_kernel,
        grid=(T // tm,),
        in_specs=[tok(RWKV_DIM), tok(ATT_DIM), tok(2 * D_MODEL), tok(D_MODEL),
                  mat(RWKV_DIM, D_MODEL), mat(ATT_DIM, D_MODEL), mat(D_MODEL, D_MODEL),
                  mat(1, D_MODEL), mat(D_MODEL, LANES), mat(1, LANES)],
        out_specs=[tok(D_MODEL), tok(D_MODEL), tok(LANES)],
        out_shape=[jax.ShapeDtypeStruct((T, D_MODEL), F32), jax.ShapeDtypeStruct((T, D_MODEL), BF16),
                   jax.ShapeDtypeStruct((T, LANES), F32)],
        compiler_params=_cparams("parallel"),
        name="merge_router",
    )(o_r, o_a, pg, x, lp['w_br_r'].astype(BF16), lp['w_br_a'].astype(BF16), lp['w_out'].astype(BF16),
      lp['norm2_g'].reshape(1, -1), rw, rb)


def _moe_kernel(be_ref, nact_ref, x_ref, gate_ref, wgu_ref, bgu_ref, wdn_ref, bdn_ref, o_ref,
                wgu_sc, wdn_sc):
    i = pl.program_id(0)
    active = i < nact_ref[0]
    new_expert = jnp.logical_or(i == 0, be_ref[i] != be_ref[jnp.maximum(i - 1, 0)])

    @pl.when(jnp.logical_and(active, new_expert))
    def _():
        for dst, src in ((wgu_sc, wgu_ref), (wdn_sc, wdn_ref)):
            for r0 in range(0, src.shape[0], MXU_DIM):
                dst[r0:r0 + MXU_DIM, :] = src[r0:r0 + MXU_DIM, :].astype(BF16)

    @pl.when(active)
    def _():
        gu = jnp.dot(x_ref[...], wgu_sc[...], preferred_element_type=F32) + bgu_ref[...]
        d_ff = gu.shape[1] // 2
        x_glu = jnp.minimum(gu[:, :d_ff], SWIGLU_LIMIT)
        x_lin = jnp.clip(gu[:, d_ff:], -SWIGLU_LIMIT, SWIGLU_LIMIT)
        act = x_glu * jax.nn.sigmoid(SWIGLU_ALPHA * x_glu) * (x_lin + 1.0)
        out = jnp.dot(act.astype(BF16), wdn_sc[...], preferred_element_type=F32) + bdn_ref[...]
        o_ref[...] = out * gate_ref[...]

    @pl.when(i >= nact_ref[0])
    def _():
        o_ref[...] = jnp.zeros_like(o_ref)


def _moe_ffn_blocks(xs, row_gate, blk_e, n_act, w_gu, b_gu, w_dn, b_dn):
    n_rows, D = xs.shape
    n_blk = n_rows // MOE_BLOCK
    d_ff2 = w_gu.shape[2]
    grid_spec = pltpu.PrefetchScalarGridSpec(
        num_scalar_prefetch=2,
        grid=(n_blk,),
        in_specs=[pl.BlockSpec((MOE_BLOCK, D), lambda i, be, na: (i, 0)),
                  pl.BlockSpec((MOE_BLOCK, 1), lambda i, be, na: (i, 0)),
                  pl.BlockSpec((None, D, d_ff2), lambda i, be, na: (be[i], 0, 0)),
                  pl.BlockSpec((None, 1, d_ff2), lambda i, be, na: (be[i], 0, 0)),
                  pl.BlockSpec((None, d_ff2 // 2, D), lambda i, be, na: (be[i], 0, 0)),
                  pl.BlockSpec((None, 1, D), lambda i, be, na: (be[i], 0, 0))],
        out_specs=pl.BlockSpec((MOE_BLOCK, D), lambda i, be, na: (i, 0)),
        scratch_shapes=[pltpu.VMEM((D, d_ff2), BF16), pltpu.VMEM((d_ff2 // 2, D), BF16)],
    )
    return pl.pallas_call(
        _moe_kernel,
        grid_spec=grid_spec,
        out_shape=jax.ShapeDtypeStruct((n_rows, D), F32),
        compiler_params=_cparams("arbitrary"),
        name="moe_ffn",
    )(blk_e, n_act, xs, row_gate.reshape(n_rows, 1), w_gu, b_gu.reshape(N_EXPERTS, 1, d_ff2),
      w_dn, b_dn.reshape(N_EXPERTS, 1, D))


def _moe(h2, logits, x1, moe_w):
    T, D = h2.shape
    top_v, top_e = lax.top_k(logits, TOP_K)
    gate = jax.nn.softmax(top_v, axis=-1)
    rows = T * TOP_K
    flat_e = top_e.reshape(-1).astype(jnp.int32)
    e_sorted, order = lax.sort((flat_e, jnp.arange(rows, dtype=jnp.int32)), num_keys=1, is_stable=True)
    rank = jnp.argsort(order).astype(jnp.int32)
    bounds = jnp.searchsorted(e_sorted, jnp.arange(N_EXPERTS + 1, dtype=jnp.int32)).astype(jnp.int32)
    start, counts = bounds[:-1], bounds[1:] - bounds[:-1]
    padded = (counts + MOE_BLOCK - 1) // MOE_BLOCK * MOE_BLOCK
    pad_end = jnp.cumsum(padded)
    pad_start = pad_end - padded
    n_blk = -(-(rows + N_EXPERTS * (MOE_BLOCK - 1)) // MOE_BLOCK)
    n_rows = n_blk * MOE_BLOCK
    blk_start = jnp.arange(n_blk, dtype=jnp.int32) * MOE_BLOCK
    blk_e = jnp.minimum(jnp.sum(blk_start[:, None] >= pad_end[None, :], axis=1, dtype=jnp.int32), N_EXPERTS - 1)
    n_act = (pad_end[-1:] // MOE_BLOCK).astype(jnp.int32)
    row_off = (blk_start - pad_start[blk_e])[:, None] + jnp.arange(MOE_BLOCK, dtype=jnp.int32)[None, :]
    row_valid = row_off < counts[blk_e][:, None]
    row_flat = order[jnp.where(row_valid, start[blk_e][:, None] + row_off, 0)]
    row_tok = jnp.where(row_valid, row_flat // TOP_K, T).reshape(n_rows)
    row_gate = jnp.where(row_valid, gate.reshape(-1)[row_flat], 0.0).reshape(n_rows)
    pos = (pad_start[flat_e] + rank - start[flat_e]).reshape(T, TOP_K)

    h_pad = jnp.concatenate([h2, jnp.zeros((1, D), h2.dtype)], axis=0)
    xs = h_pad[row_tok]
    out = _moe_ffn_blocks(xs, row_gate, blk_e, n_act, *moe_w)
    y = x1
    for k in range(TOP_K):
        y = y + out[pos[:, k]]
    return y


def _trunk_layer(x, lp, w_bf, moe_w, lam, lam_init, s0, prev_row, cache=None):
    B, L, D = x.shape
    T = B * L
    x2 = x.reshape(T, D)
    tl = _tile(L, 320, BF16_ROWS)
    tm = _tile(T, 1024, BF16_ROWS) if cache is not None else _tile(L, 1024, BF16_ROWS)
    tm_small = _tile(tm, 320, BF16_ROWS)
    w_r, w_qk, w_v, w_g = w_bf

    pr = _norm_matmul(x2, lp['norm1_g'], w_r, tm_small)
    pqk = _norm_matmul(x2, lp['norm1_g'], w_qk, tm)
    pv = _norm_matmul(x2, lp['norm1_g'], w_v, tm)
    pg = _norm_matmul(x2, lp['norm1_g'], w_g, tm)

    pr3 = pr.reshape(B, L, RWKV_PROJ)
    tile_last = pr3.reshape(B, L // tl, tl, RWKV_PROJ)[:, :-1, -1]
    prev_rows = jnp.concatenate([prev_row.astype(F32), tile_last], axis=1).reshape(T // tl, 1, RWKV_PROJ)
    *scan_ops, bonus, g = _rwkv_prep(pr, prev_rows, lp, B, L, tl)
    nc = B * RWKV_HEADS
    lane_chain = _scan_lane_order(nc)
    s0c = s0.astype(F32).transpose(3, 2, 0, 1).reshape(RWKV_HEAD, RWKV_HEAD, nc)[:, :, lane_chain]
    y, s_fin_c = _rwkv_scan(scan_ops, s0c, _tile(L, SCAN_TT, 2))
    s_fin_c = s_fin_c[:, :, np.argsort(lane_chain)]
    s_fin = s_fin_c.reshape(RWKV_HEAD, RWKV_HEAD, B, RWKV_HEADS).transpose(2, 3, 1, 0)
    o_r = _rwkv_post(y, bonus, g, lp, B, L, tl)

    if cache is None:
        assert ATT_TK % ATT_TQ == 0
        lpad = -(-L // ATT_TK) * ATT_TK
        k_rows, qb, kb, vb = _qk_norm(pqk.reshape(B, L, 2 * ATT_QK), pv.reshape(B, L, ATT_DIM), lp, lpad,
                                      _tile(lpad, 512, BF16_ROWS))
        o_a = _attention(qb, kb, vb, lam, lp['subln_g'], lam_init, chunked=True, q_pos0=0, n_keys=L,
                         n_q=-(-L // ATT_TQ), tq=ATT_TQ, tk=ATT_TK)
    else:
        cache_k, cache_v = cache
        past = cache_k.shape[1]
        k_rows, qb, kb, vb = _qk_norm(pqk.reshape(B, L, 2 * ATT_QK), pv.reshape(B, L, ATT_DIM), lp, L, L)
        n_keys = past + L
        lk_pad = -(-n_keys // ATT_TK) * ATT_TK
        zk = jnp.zeros((B, lk_pad - n_keys, ATT_QK), BF16)
        k_all = jnp.concatenate([cache_k.reshape(B, past, ATT_QK).astype(BF16), kb, zk], axis=1)
        v_all = jnp.concatenate([cache_v.reshape(B, past, ATT_DIM).astype(BF16), vb, zk], axis=1)
        tq = -(-L // LANES) * LANES
        qb = jnp.concatenate([qb, jnp.zeros((B, tq - L, ATT_QK), BF16)], axis=1)
        o_a = _attention(qb, k_all, v_all, lam, lp['subln_g'], lam_init, chunked=False, q_pos0=past,
                         n_keys=n_keys, n_q=1, tq=tq, tk=ATT_TK)
    o_a = o_a[:, :L].reshape(T, ATT_DIM)

    x1, h2, logits = _merge(o_r, o_a, pg, x2, lp, tm_small)
    x_out = _moe(h2, logits[:, :N_EXPERTS], x1, moe_w)
    return (x_out.reshape(B, L, D), k_rows.reshape(B, L, ATT_HEADS, 2, QK_DIM),
            pv.reshape(B, L, ATT_HEADS, V_DIM), s_fin, pr3[:, -1:])


def kernel(x_prompt, x_sample, cache_k, cache_v, state_wkv, state_shift, meta, norm1_g, w_in, shift_mu, w0, w_up, a0, a_up, g_up, k_k, k_a, r_k, lnx_g, lnx_b, q_norm_g, k_norm_g, lam_q1, lam_k1, lam_q2, lam_k2, subln_g, w_br_r, w_br_a, w_out, norm2_g, router_w, router_b, w_gu, b_gu, w_dn, b_dn):
    depth = w_in.shape[0]
    n_prompt = x_prompt.shape[0]
    meta_rows = jnp.broadcast_to(meta.astype(x_prompt.dtype)[None], (n_prompt, N_META, D_MODEL))
    xp = jnp.concatenate([meta_rows, x_prompt], axis=1)
    xs = x_sample
    st_p, st_s = [], []
    for l in range(depth):
        lp = dict(norm1_g=norm1_g[l], shift_mu=shift_mu[l], w0=w0[l], w_up=w_up[l],
                  a0=a0[l], a_up=a_up[l], g_up=g_up[l], k_k=k_k[l], k_a=k_a[l], r_k=r_k[l].reshape(-1),
                  lnx_g=lnx_g[l], lnx_b=lnx_b[l], q_norm_g=q_norm_g[l], k_norm_g=k_norm_g[l],
                  subln_g=subln_g[l], w_br_r=w_br_r[l], w_br_a=w_br_a[l], w_out=w_out[l],
                  norm2_g=norm2_g[l], router_w=router_w[l], router_b=router_b[l])
        wl = w_in[l].astype(BF16)
        c0, c1, c2 = RWKV_PROJ, RWKV_PROJ + 2 * ATT_QK, RWKV_PROJ + 2 * ATT_QK + ATT_DIM
        w_bf = (wl[:, :c0], wl[:, c0:c1], wl[:, c1:c2], wl[:, c2:])
        moe_w = (w_gu[l].astype(F32), b_gu[l].astype(F32), w_dn[l].astype(F32), b_dn[l].astype(F32))
        lam_init = 0.8 - 0.6 * math.exp(-0.3 * l)
        lam = (jnp.exp(jnp.sum(lam_q1[l].astype(F32) * lam_k1[l].astype(F32)))
               - jnp.exp(jnp.sum(lam_q2[l].astype(F32) * lam_k2[l].astype(F32))) + lam_init)
        s0 = jnp.zeros((n_prompt, RWKV_HEADS, RWKV_HEAD, RWKV_HEAD), F32)
        r0 = jnp.zeros((n_prompt, 1, RWKV_PROJ), F32)
        xp, kp, vp, sp, rp = _trunk_layer(xp, lp, w_bf, moe_w, lam, lam_init, s0, r0)
        xs, ks, vs, ss, rs = _trunk_layer(xs, lp, w_bf, moe_w, lam, lam_init, state_wkv[l], state_shift[l],
                                          cache=(cache_k[l], cache_v[l]))
        st_p.append((kp, vp, sp, rp))
        st_s.append((ks, vs, ss, rs))
    stack = lambda st, j: jnp.stack([t[j] for t in st])
    return (xp[:, N_META:], xs,
            stack(st_p, 0), stack(st_p, 1), stack(st_p, 2), stack(st_p, 3),
            stack(st_s, 0), stack(st_s, 1), stack(st_s, 2), stack(st_s, 3))
```

```python
import functools
import math

import jax
import jax.numpy as jnp
import numpy as np
from jax import lax
from jax.experimental import pallas as pl
from jax.experimental.pallas import tpu as pltpu

F32 = jnp.float32
BF16 = jnp.bfloat16

D_MODEL = 1024
CHUNK = 64
CHUNK_SHIFT = CHUNK.bit_length() - 1
assert 1 << CHUNK_SHIFT == CHUNK
N_META = 16
RWKV_HEAD = 64
RWKV_HEADS = D_MODEL // RWKV_HEAD
RWKV_DIM = RWKV_HEADS * RWKV_HEAD
DECAY_LORA = 64
AAA_LORA = 64
GATE_LORA = 128
RWKV_PROJ = 3 * RWKV_DIM + DECAY_LORA + AAA_LORA + GATE_LORA
LNX_EPS = 64e-5
ATT_HEADS = 8
QK_DIM = 64
V_DIM = 2 * QK_DIM
ATT_QK = ATT_HEADS * 2 * QK_DIM
ATT_DIM = ATT_HEADS * V_DIM
N_EXPERTS = 32
TOP_K = 4
SWIGLU_ALPHA = 1.702
SWIGLU_LIMIT = 7.0
RMS_EPS = 1e-6
NEG_INF = -1e30
LOG2E = math.log2(math.e)
Q_SCALE = QK_DIM ** -0.5 * LOG2E


def _bf16_terms(x, n):
    terms = []
    for _ in range(n):
        t = float(np.float32(x).astype(jnp.bfloat16))
        terms.append(t)
        x -= t
    return tuple(terms)


LOG2E_BF16_TERMS = _bf16_terms(LOG2E, 3)

LANES = 128
SUBLANES = 8
BF16_ROWS = 16
BF16_EXACT_INT = 256
MXU_DIM = 256
VMEM_LIMIT = 56 * 1024 * 1024

GROUP_BLOCK = MXU_DIM
MOE_BLOCK = 512
ATT_TQ = 256
ATT_TK = 512
ATT_SUB = 256
SCAN_TT = 24
SCAN_ACCUMULATORS = 4


def _cparams(*sem):
    return pltpu.CompilerParams(dimension_semantics=sem, vmem_limit_bytes=VMEM_LIMIT)


def _tile(n, target, mult):
    best = None
    for t in range(mult, min(n, target) + 1, mult):
        if n % t == 0:
            best = t
    return best if best is not None else n


def _group_ones():
    g = np.arange(GROUP_BLOCK) // RWKV_HEAD
    return jnp.asarray((g[:, None] == g[None, :]).astype(np.float32), BF16)


def _gsum(x, ones_ref):
    hi = x.astype(BF16)
    lo = (x - hi.astype(F32)).astype(BF16)
    ones = ones_ref[...]
    outs = []
    for c in range(x.shape[1] // GROUP_BLOCK):
        sl = slice(c * GROUP_BLOCK, (c + 1) * GROUP_BLOCK)
        outs.append(jnp.dot(hi[:, sl], ones, preferred_element_type=F32)
                    + jnp.dot(lo[:, sl], ones, preferred_element_type=F32))
    return jnp.concatenate(outs, axis=1)


def _norm_matmul_kernel(x_ref, g_ref, w_ref, o_ref):
    x = x_ref[...]
    h = x * lax.rsqrt(jnp.mean(x * x, axis=-1, keepdims=True) + RMS_EPS) * g_ref[...]
    o_ref[...] = jnp.dot(h.astype(BF16), w_ref[...], preferred_element_type=F32)


def _norm_matmul(x, g, w, tm):
    T, D = x.shape
    N = w.shape[1]
    return pl.pallas_call(
        _norm_matmul_kernel,
        grid=(T // tm,),
        in_specs=[pl.BlockSpec((tm, D), lambda i: (i, 0)),
                  pl.BlockSpec((1, D), lambda i: (0, 0)),
                  pl.BlockSpec((D, N), lambda i: (0, 0))],
        out_specs=pl.BlockSpec((tm, N), lambda i: (i, 0)),
        out_shape=jax.ShapeDtypeStruct((T, N), F32),
        compiler_params=_cparams("parallel"),
        name="norm_matmul",
    )(x, g.reshape(1, D), w)


def _rwkv_prep_kernel(pr_ref, prev_ref, mu_ref, w0_ref, wup_ref, a0_ref, aup_ref, gup_ref,
                      kk_ref, ka_ref, rk_ref, ones_ref,
                      kn_o, d_o, b_o, k_o, v_o, r_o, bonus_o, g_o):
    pr = pr_ref[...]
    row = lax.broadcasted_iota(jnp.int32, pr.shape, 0)
    prev = jnp.where(row == 0, prev_ref[...], pltpu.roll(pr, 1, axis=0))
    xs = pr + (prev - pr) * mu_ref[...]
    r = xs[:, :RWKV_DIM]
    k = xs[:, RWKV_DIM:2 * RWKV_DIM]
    v = xs[:, 2 * RWKV_DIM:3 * RWKV_DIM]
    lora_in = xs[:, 3 * RWKV_DIM:3 * RWKV_DIM + DECAY_LORA + AAA_LORA]
    gd = xs[:, 3 * RWKV_DIM + DECAY_LORA + AAA_LORA:]
    wl = jnp.dot(jnp.tanh(lora_in).astype(BF16), wup_ref[...], preferred_element_type=F32)
    al = jnp.dot(lora_in.astype(BF16), aup_ref[...], preferred_element_type=F32)
    z = -(w0_ref[...] + wl)
    w = -(jnp.maximum(z, 0.0) + jnp.log1p(jnp.exp(-jnp.abs(z)))) - 0.5
    decay = jnp.exp(-jnp.exp(w))
    a = jax.nn.sigmoid(a0_ref[...] + al)
    g = jnp.dot(jax.nn.sigmoid(gd).astype(BF16), gup_ref[...], preferred_element_type=F32)
    kk = k * kk_ref[...]
    nrm = jnp.sqrt(_gsum(kk * kk, ones_ref))
    kk = kk / jnp.maximum(nrm, 1e-12)
    k = k * (1.0 + (a - 1.0) * ka_ref[...])
    slab = lambda val: val.reshape(val.shape[0], RWKV_DIM // LANES, LANES)
    kn_o[...] = slab(-kk)
    d_o[...] = slab(decay)
    b_o[...] = slab(kk * a)
    k_o[...] = slab(k)
    v_o[...] = slab(v)
    r_o[...] = slab(r)
    bonus_o[...] = _gsum(r * k * rk_ref[...], ones_ref) * v
    g_o[...] = g


def _rwkv_prep(pr, prev_rows, lp, B, L, tl):
    nt = L // tl
    rows_per_token = RWKV_DIM // LANES
    row = lambda a: a.reshape(1, -1).astype(F32)
    zpad = jnp.zeros((DECAY_LORA, RWKV_DIM), F32)
    wup = jnp.concatenate([lp['w_up'], zpad], axis=0).astype(BF16)
    aup = jnp.concatenate([zpad, lp['a_up']], axis=0).astype(BF16)
    vec = lambda n: pl.BlockSpec((1, n), lambda b, i: (0, 0))
    mat = lambda m, n: pl.BlockSpec((m, n), lambda b, i: (0, 0))
    tok = pl.BlockSpec((tl, RWKV_DIM), lambda b, i: (b * nt + i, 0))
    time_major = pl.BlockSpec((tl, rows_per_token, LANES), lambda b, i: (i, b, 0))
    return pl.pallas_call(
        _rwkv_prep_kernel,
        grid=(B, nt),
        in_specs=[pl.BlockSpec((tl, RWKV_PROJ), lambda b, i: (b * nt + i, 0)),
                  pl.BlockSpec((None, 1, RWKV_PROJ), lambda b, i: (b * nt + i, 0, 0)),
                  vec(RWKV_PROJ), vec(RWKV_DIM), mat(2 * DECAY_LORA, RWKV_DIM), vec(RWKV_DIM),
                  mat(2 * AAA_LORA, RWKV_DIM), mat(GATE_LORA, RWKV_DIM),
                  vec(RWKV_DIM), vec(RWKV_DIM), vec(RWKV_DIM), mat(GROUP_BLOCK, GROUP_BLOCK)],
        out_specs=[time_major] * 6 + [tok] * 2,
        out_shape=[jax.ShapeDtypeStruct((L, B * rows_per_token, LANES), F32)] * 6
                  + [jax.ShapeDtypeStruct((B * L, RWKV_DIM), F32)] * 2,
        compiler_params=_cparams("parallel", "parallel"),
        name="rwkv_prep",
    )(pr, prev_rows, row(lp['shift_mu']), row(lp['w0']), wup, row(lp['a0']), aup,
      lp['g_up'].astype(BF16), row(lp['k_k']), row(lp['k_a']), row(lp['r_k']), _group_ones())


def _rwkv_scan_kernel(kn_ref, d_ref, b_ref, k_ref, v_ref, r_ref, s0_ref, y_ref, sfin_ref,
                      s_sc, opa_sc, opb_sc):
    tb = pl.program_id(1)

    @pl.when(tb == 0)
    def _():
        s_sc[...] = s0_ref[...]

    tt = kn_ref.shape[0]
    refs = (kn_ref, d_ref, b_ref, k_ref, v_ref, r_ref)

    def stage(t, op_sc):
        t = jnp.minimum(t, tt - 1)
        for i, ref in enumerate(refs):
            mt = ref[t].T
            op_sc[i] = jnp.concatenate([mt[:RWKV_HEAD], mt[RWKV_HEAD:]], axis=1)

    def tree_sum(terms):
        while len(terms) > 1:
            terms = [a + b for a, b in zip(terms[::2], terms[1::2])]
        return terms[0]

    class Sums:
        def __init__(self):
            self.partial = [None] * SCAN_ACCUMULATORS

        def add(self, k, term):
            a = k % SCAN_ACCUMULATORS
            self.partial[a] = term if self.partial[a] is None else self.partial[a] + term

        def total(self):
            return tree_sum(self.partial)

    def state_dot(op_sc):
        acc = Sums()
        for k in range(RWKV_HEAD):
            acc.add(k, s_sc[k] * op_sc[0, k:k + 1, :])
        return acc.total()

    def step(t, op_sc, next_sc, sa):
        row = lambda i, k: op_sc[i, k:k + 1, :]
        vv = op_sc[4]
        y_acc, sa_acc = Sums(), Sums()
        for k in range(RWKV_HEAD):
            sn = s_sc[k] * row(1, k) + sa * row(2, k) + vv * row(3, k)
            s_sc[k] = sn
            y_acc.add(k, sn * row(5, k))
            sa_acc.add(k, sn * next_sc[0, k:k + 1, :])
        y = y_acc.total()
        y_ref[t] = jnp.concatenate([y[:, :RWKV_HEAD], y[:, RWKV_HEAD:]], axis=0).T
        return sa_acc.total()

    stage(0, opa_sc)

    def pair(i, sa):
        t0 = 2 * i
        stage(t0 + 1, opb_sc)
        sa = step(t0, opa_sc, opb_sc, sa)
        stage(t0 + 2, opa_sc)
        return step(t0 + 1, opb_sc, opa_sc, sa)

    lax.fori_loop(0, tt // 2, pair, state_dot(opa_sc))

    @pl.when(tb == pl.num_programs(1) - 1)
    def _():
        sfin_ref[...] = s_sc[...]


def _scan_lane_order(nc):
    lane = np.arange(nc)
    return (lane // LANES) * LANES + 2 * (lane % RWKV_HEAD) + (lane % LANES) // RWKV_HEAD


def _rwkv_scan(ops, s0, tt):
    L, pairs, _ = ops[0].shape
    n = RWKV_HEAD
    nc = 2 * pairs
    assert tt % 2 == 0 and L % tt == 0 and nc % LANES == 0
    tspec = pl.BlockSpec((tt, n, LANES), lambda c, t: (t, c, 0))
    sspec = pl.BlockSpec((n, n, LANES), lambda c, t: (0, 0, c))
    return pl.pallas_call(
        _rwkv_scan_kernel,
        grid=(nc // LANES, L // tt),
        in_specs=[tspec] * 6 + [sspec],
        out_specs=[tspec, sspec],
        out_shape=[jax.ShapeDtypeStruct((L, pairs, LANES), F32), jax.ShapeDtypeStruct((n, n, nc), F32)],
        scratch_shapes=[pltpu.VMEM((n, n, LANES), F32), pltpu.VMEM((6, n, LANES), F32),
                        pltpu.VMEM((6, n, LANES), F32)],
        compiler_params=_cparams("parallel", "arbitrary"),
        name="rwkv_scan",
    )(*ops, s0)


def _rwkv_post_kernel(y_ref, bonus_ref, g_ref, lg_ref, lb_ref, ones_ref, o_ref):
    y = jnp.concatenate([y_ref[:, j, :] for j in range(y_ref.shape[1])], axis=1)
    inv_n = 1.0 / RWKV_HEAD
    mean = _gsum(y, ones_ref) * inv_n
    yc = y - mean
    var = _gsum(yc * yc, ones_ref) * inv_n
    yn = yc * lax.rsqrt(var + LNX_EPS) * lg_ref[...] + lb_ref[...]
    o_ref[...] = ((yn + bonus_ref[...]) * g_ref[...]).astype(BF16)


def _rwkv_post(y, bonus, g, lp, B, L, tl):
    nt = L // tl
    tok = pl.BlockSpec((tl, RWKV_DIM), lambda b, i: (b * nt + i, 0))
    vec = pl.BlockSpec((1, RWKV_DIM), lambda b, i: (0, 0))
    return pl.pallas_call(
        _rwkv_post_kernel,
        grid=(B, nt),
        in_specs=[pl.BlockSpec((tl, RWKV_DIM // LANES, LANES), lambda b, i: (i, b, 0)), tok, tok, vec, vec,
                  pl.BlockSpec((GROUP_BLOCK, GROUP_BLOCK), lambda b, i: (0, 0))],
        out_specs=tok,
        out_shape=jax.ShapeDtypeStruct((B * L, RWKV_DIM), BF16),
        compiler_params=_cparams("parallel", "parallel"),
        name="rwkv_post",
    )(y, bonus, g, lp['lnx_g'].reshape(1, -1), lp['lnx_b'].reshape(1, -1), _group_ones())


def _qk_norm_kernel(qk_ref, v_ref, qg_ref, kg_ref, ones_ref, kout_ref, qb_ref, kb_ref, vb_ref, *, length):
    tl = qk_ref.shape[0]
    valid = (pl.program_id(1) * tl + lax.broadcasted_iota(jnp.int32, (tl, 1), 0)) < length
    q = qk_ref[:, :ATT_QK]
    k = qk_ref[:, ATT_QK:]
    inv_n = 1.0 / QK_DIM
    qn = q * lax.rsqrt(_gsum(q * q, ones_ref) * inv_n + RMS_EPS) * qg_ref[...]
    kn = k * lax.rsqrt(_gsum(k * k, ones_ref) * inv_n + RMS_EPS) * kg_ref[...]
    kout_ref[...] = kn
    qb_ref[...] = jnp.where(valid, qn * Q_SCALE, 0.0).astype(BF16)
    kb_ref[...] = jnp.where(valid, kn, 0.0).astype(BF16)
    vb_ref[...] = jnp.where(valid, v_ref[...], 0.0).astype(BF16)


def _qk_norm(pqk, pv, lp, lpad, tl):
    B, L, _ = pqk.shape
    tile_g = lambda g: jnp.tile(g.astype(F32), ATT_QK // QK_DIM).reshape(1, ATT_QK)
    blk = lambda n: pl.BlockSpec((None, tl, n), lambda b, i: (b, i, 0))
    vec = pl.BlockSpec((1, ATT_QK), lambda b, i: (0, 0))
    bshape = jax.ShapeDtypeStruct((B, lpad, ATT_QK), BF16)
    return pl.pallas_call(
        functools.partial(_qk_norm_kernel, length=L),
        grid=(B, lpad // tl),
        in_specs=[blk(2 * ATT_QK), blk(ATT_DIM), vec, vec,
                  pl.BlockSpec((GROUP_BLOCK, GROUP_BLOCK), lambda b, i: (0, 0))],
        out_specs=[blk(ATT_QK)] * 4,
        out_shape=[jax.ShapeDtypeStruct((B, L, ATT_QK), F32), bshape, bshape, bshape],
        compiler_params=_cparams("parallel", "parallel"),
        name="qk_norm",
    )(pqk, pv, tile_g(lp['q_norm_g']), tile_g(lp['k_norm_g']), _group_ones())


def _attn_kernel(slope_ref, lam_ref, q_ref, k_ref, kpos_ref, v_ref, g_ref, o_ref,
                 qt_sc, vt_sc, pen_sc, sa_sc, sb_sc, m_sc, a_sc,
                 *, chunked, q_pos0, n_keys, out_scale):
    tq = q_ref.shape[0]
    n_tiles, _, tk = vt_sc.shape
    h = pl.program_id(1)
    i = pl.program_id(2)

    @pl.when(i == 0)
    def _():
        ones_rows = (lax.broadcasted_iota(jnp.int32, (BF16_ROWS, tk), 0) == 0).astype(F32).astype(BF16)

        def transpose_tile(j, carry):
            k0 = pl.multiple_of(j * tk, tk)
            vt_sc[j, :V_DIM, :] = v_ref[pl.ds(k0, tk), :].astype(F32).T.astype(BF16)
            vt_sc[j, V_DIM:, :] = ones_rows
            return carry
        lax.fori_loop(0, n_tiles, transpose_tile, 0)

    qt = q_ref[...].astype(F32).T
    sub = lax.broadcasted_iota(jnp.int32, qt.shape, 0)
    slope = slope_ref[h]
    coef = jnp.zeros(qt.shape, F32)
    for r, c in enumerate([w * c for w in (CHUNK, 1) for c in LOG2E_BF16_TERMS]):
        coef = jnp.where(sub == r, slope * c, coef)
    coef = coef.astype(BF16)
    for mi, keep in enumerate((sub < QK_DIM, sub >= QK_DIM)):
        qt_sc[mi, :2 * QK_DIM, :] = jnp.where(keep, qt, 0.0).astype(BF16)
        qt_sc[mi, 2 * QK_DIM:, :] = coef
    q_first = q_pos0 + i * tq
    qpos = q_first + lax.broadcasted_iota(jnp.int32, (1, tq), 1)
    if chunked:
        chunk_end = lambda p: N_META + CHUNK + (((p - N_META) >> CHUNK_SHIFT) << CHUNK_SHIFT)
        vis = jnp.where(qpos < N_META, N_META, chunk_end(qpos))
        vis_last = chunk_end(q_first + tq - 1)
        n_kt = (jnp.minimum(vis_last, n_keys) + tk - 1) // tk
    else:
        vis = jnp.full((1, tq), n_keys, jnp.int32)
        n_kt = (n_keys + tk - 1) // tk
    n_past = jnp.minimum((q_first + 1) // tk, n_kt)

    m_sc[...] = jnp.full(m_sc.shape, -jnp.inf, F32)
    a_sc[...] = jnp.zeros(a_sc.shape, F32)

    def accumulate(mi, s, vtj):
        m_old = m_sc[mi]
        m_new = jnp.maximum(m_old, jnp.max(s, axis=0, keepdims=True))
        p = jnp.exp2(s - m_new)
        a_sc[mi] = jnp.exp2(m_old - m_new) * a_sc[mi] + jnp.dot(vtj, p.astype(BF16),
                                                                preferred_element_type=F32)
        m_sc[mi] = m_new

    def scores_into(s_ref, j):
        k0 = pl.multiple_of(jnp.minimum(j, n_tiles - 1) * tk, tk)
        kj = jnp.concatenate([k_ref[pl.ds(k0, tk), :], kpos_ref[pl.ds(k0, tk), :]], axis=1)
        for mi in range(2):
            s_ref[mi] = jnp.dot(kj, qt_sc[mi], preferred_element_type=F32)

    def consume(s_ref, j, pen_ref=None):
        for k0 in range(0, tk, ATT_SUB):
            for mi in range(2):
                s = s_ref[mi, k0:k0 + ATT_SUB, :]
                if pen_ref is not None:
                    s = s - pen_ref[k0:k0 + ATT_SUB, :]
                accumulate(mi, s, vt_sc[j, :, k0:k0 + ATT_SUB])

    n_pairs = n_past // 2

    @pl.when(n_pairs > 0)
    def _():
        scores_into(sa_sc, 0)

    def pair_body(jj, carry):
        j0 = 2 * jj
        scores_into(sb_sc, j0 + 1)
        consume(sa_sc, j0)
        scores_into(sa_sc, j0 + 2)
        consume(sb_sc, j0 + 1)
        return carry

    lax.fori_loop(0, n_pairs, pair_body, 0)

    def past_single(j, carry):
        scores_into(sa_sc, j)
        consume(sa_sc, j)
        return carry

    lax.fori_loop(2 * n_pairs, n_past, past_single, 0)

    def edge_body(j, carry):
        kpos = j * tk + lax.broadcasted_iota(jnp.int32, (tk, 1), 0)
        ahead = jnp.maximum(kpos - qpos, 0).astype(F32)
        pen_sc[...] = jnp.where(kpos < vis, (2.0 * LOG2E * slope) * ahead, -NEG_INF)
        scores_into(sa_sc, j)
        consume(sa_sc, j, pen_sc)
        return carry

    lax.fori_loop(n_past, n_kt, edge_body, 0)

    num = [a_sc[mi, :V_DIM, :] / a_sc[mi, V_DIM:V_DIM + 1, :] for mi in range(2)]
    o = num[0] - lam_ref[0] * num[1]
    o = o * lax.rsqrt(jnp.mean(o * o, axis=0, keepdims=True) + RMS_EPS) * g_ref[...] * out_scale
    o_ref[...] = o.T.astype(BF16)


def _attention(qb, kb, vb, lam, subln_g, lam_init, *, chunked, q_pos0, n_keys, n_q, tq, tk):
    B = qb.shape[0]
    lk = kb.shape[1]
    slopes = jnp.asarray([2.0 ** (-8.0 * (i + 1) / ATT_HEADS) for i in range(ATT_HEADS)], F32)
    gain = jnp.broadcast_to(subln_g.astype(F32).reshape(V_DIM, 1), (V_DIM, tq))
    assert lk // CHUNK <= BF16_EXACT_INT
    kpos = np.arange(lk)
    digits = np.zeros((lk, V_DIM), np.float32)
    digits[:, 0:3] = (kpos // CHUNK)[:, None]
    digits[:, 3:6] = (kpos % CHUNK)[:, None]
    kpos_digits = jnp.asarray(digits, BF16)
    smem = pl.BlockSpec(memory_space=pltpu.SMEM)
    qspec = pl.BlockSpec((None, tq, V_DIM), lambda b, h, i: (b, i, h))
    kspec = pl.BlockSpec((None, lk, V_DIM), lambda b, h, i: (b, 0, h))
    const = lambda shape: pl.BlockSpec(shape, lambda b, h, i: (0, 0))
    acc_rows = V_DIM + BF16_ROWS
    return pl.pallas_call(
        functools.partial(_attn_kernel, chunked=chunked, q_pos0=q_pos0, n_keys=n_keys,
                          out_scale=1.0 - lam_init),
        grid=(B, ATT_HEADS, n_q),
        in_specs=[smem, smem, qspec, kspec, const((lk, V_DIM)), kspec, const((V_DIM, tq))],
        out_specs=qspec,
        out_shape=jax.ShapeDtypeStruct((B, n_q * tq, ATT_DIM), BF16),
        scratch_shapes=[pltpu.VMEM((2, 2 * V_DIM, tq), BF16), pltpu.VMEM((lk // tk, acc_rows, tk), BF16),
                        pltpu.VMEM((tk, tq), F32),
                        pltpu.VMEM((2, tk, tq), F32), pltpu.VMEM((2, tk, tq), F32),
                        pltpu.VMEM((2, 1, tq), F32),
                        pltpu.VMEM((2, acc_rows, tq), F32)],
        compiler_params=_cparams("parallel", "parallel", "arbitrary"),
        name="diff_attention",
    )(slopes, lam.reshape(1).astype(F32), qb, kb, kpos_digits, vb, gain)


def _merge_kernel(or_ref, oa_ref, pg_ref, x_ref, wr_ref, wa_ref, wo_ref, g2_ref, rw_ref, rb_ref,
                  x1_ref, h2_ref, lg_ref):
    gates = jax.nn.sigmoid(pg_ref[...])
    br = jnp.dot(or_ref[...], wr_ref[...], preferred_element_type=F32)
    ba = jnp.dot(oa_ref[...], wa_ref[...], preferred_element_type=F32)
    merged = gates[:, :D_MODEL] * br + gates[:, D_MODEL:] * ba
    x1 = x_ref[...] + jnp.dot(merged.astype(BF16), wo_ref[...], preferred_element_type=F32)
    h2 = x1 * lax.rsqrt(jnp.mean(x1 * x1, axis=-1, keepdims=True) + RMS_EPS) * g2_ref[...]
    x1_ref[...] = x1
    h2_ref[...] = h2.astype(BF16)
    h_hi = h2.astype(BF16)
    h_lo = (h2 - h_hi.astype(F32)).astype(BF16)
    w_hi, w_lo = rw_ref[0], rw_ref[1]
    lg_ref[...] = (jnp.dot(h_hi, w_hi, preferred_element_type=F32)
                   + jnp.dot(h_hi, w_lo, preferred_element_type=F32)
                   + jnp.dot(h_lo, w_hi, preferred_element_type=F32)) + rb_ref[...]


def _merge(o_r, o_a, pg, x, lp, tm):
    T = x.shape[0]
    rw = jnp.zeros((D_MODEL, LANES), F32).at[:, :N_EXPERTS].set(lp['router_w'].astype(F32))
    rw_hi = rw.astype(BF16)
    rw = jnp.stack([rw_hi, (rw - rw_hi.astype(F32)).astype(BF16)])
    rb = jnp.zeros((1, LANES), F32).at[0, :N_EXPERTS].set(lp['router_b'].astype(F32))
    tok = lambda n: pl.BlockSpec((tm, n), lambda i: (i, 0))
    mat = lambda m, n: pl.BlockSpec((m, n), lambda i: (0, 0))
    return pl.pallas_call(
        _merge_kernel,
        grid=(T // tm,),
        in_specs=[tok(RWKV_DIM), tok(ATT_DIM), tok(2 * D_MODEL), tok(D_MODEL),
                  mat(RWKV_DIM, D_MODEL), mat(ATT_DIM, D_MODEL), mat(D_MODEL, D_MODEL),
                  mat(1, D_MODEL), pl.BlockSpec((2, D_MODEL, LANES), lambda i: (0, 0, 0)), mat(1, LANES)],
        out_specs=[tok(D_MODEL), tok(D_MODEL), tok(LANES)],
        out_shape=[jax.ShapeDtypeStruct((T, D_MODEL), F32), jax.ShapeDtypeStruct((T, D_MODEL), BF16),
                   jax.ShapeDtypeStruct((T, LANES), F32)],
        compiler_params=_cparams("parallel"),
        name="merge_router",
    )(o_r, o_a, pg, x, lp['w_br_r'].astype(BF16), lp['w_br_a'].astype(BF16), lp['w_out'].astype(BF16),
      lp['norm2_g'].reshape(1, -1), rw, rb)


def _moe_kernel(be_ref, nact_ref, x_ref, gate_ref, wgu_ref, bgu_ref, wdn_ref, bdn_ref, o_ref,
                wgu_sc, wdn_sc):
    i = pl.program_id(0)
    active = i < nact_ref[0]
    new_expert = jnp.logical_or(i == 0, be_ref[i] != be_ref[jnp.maximum(i - 1, 0)])

    @pl.when(jnp.logical_and(active, new_expert))
    def _():
        for dst, src in ((wgu_sc, wgu_ref), (wdn_sc, wdn_ref)):
            for r0 in range(0, src.shape[0], MXU_DIM):
                dst[r0:r0 + MXU_DIM, :] = src[r0:r0 + MXU_DIM, :].astype(BF16)

    @pl.when(active)
    def _():
        gu = jnp.dot(x_ref[...], wgu_sc[...], preferred_element_type=F32) + bgu_ref[...]
        d_ff = gu.shape[1] // 2
        x_glu = jnp.minimum(gu[:, :d_ff], SWIGLU_LIMIT)
        x_lin = jnp.clip(gu[:, d_ff:], -SWIGLU_LIMIT, SWIGLU_LIMIT)
        act = x_glu * jax.nn.sigmoid(SWIGLU_ALPHA * x_glu) * (x_lin + 1.0)
        out = jnp.dot(act.astype(BF16), wdn_sc[...], preferred_element_type=F32) + bdn_ref[...]
        o_ref[...] = out * gate_ref[...]

    @pl.when(i >= nact_ref[0])
    def _():
        o_ref[...] = jnp.zeros_like(o_ref)


def _moe_ffn_blocks(xs, row_gate, blk_e, n_act, w_gu, b_gu, w_dn, b_dn):
    n_rows, D = xs.shape
    n_blk = n_rows // MOE_BLOCK
    d_ff2 = w_gu.shape[2]
    grid_spec = pltpu.PrefetchScalarGridSpec(
        num_scalar_prefetch=2,
        grid=(n_blk,),
        in_specs=[pl.BlockSpec((MOE_BLOCK, D), lambda i, be, na: (i, 0)),
                  pl.BlockSpec((MOE_BLOCK, 1), lambda i, be, na: (i, 0)),
                  pl.BlockSpec((None, D, d_ff2), lambda i, be, na: (be[i], 0, 0)),
                  pl.BlockSpec((None, 1, d_ff2), lambda i, be, na: (be[i], 0, 0)),
                  pl.BlockSpec((None, d_ff2 // 2, D), lambda i, be, na: (be[i], 0, 0)),
                  pl.BlockSpec((None, 1, D), lambda i, be, na: (be[i], 0, 0))],
        out_specs=pl.BlockSpec((MOE_BLOCK, D), lambda i, be, na: (i, 0)),
        scratch_shapes=[pltpu.VMEM((D, d_ff2), BF16), pltpu.VMEM((d_ff2 // 2, D), BF16)],
    )
    return pl.pallas_call(
        _moe_kernel,
        grid_spec=grid_spec,
        out_shape=jax.ShapeDtypeStruct((n_rows, D), F32),
        compiler_params=_cparams("arbitrary"),
        name="moe_ffn",
    )(blk_e, n_act, xs, row_gate.reshape(n_rows, 1), w_gu, b_gu.reshape(N_EXPERTS, 1, d_ff2),
      w_dn, b_dn.reshape(N_EXPERTS, 1, D))


def _moe(h2, logits, x1, moe_w):
    T, D = h2.shape
    top_v, top_e = lax.top_k(logits, TOP_K)
    gate = jax.nn.softmax(top_v, axis=-1)
    rows = T * TOP_K
    flat_e = top_e.reshape(-1).astype(jnp.int32)
    e_sorted, order = lax.sort((flat_e, jnp.arange(rows, dtype=jnp.int32)), num_keys=1, is_stable=True)
    rank = jnp.argsort(order).astype(jnp.int32)
    bounds = jnp.searchsorted(e_sorted, jnp.arange(N_EXPERTS + 1, dtype=jnp.int32)).astype(jnp.int32)
    start, counts = bounds[:-1], bounds[1:] - bounds[:-1]
    padded = (counts + MOE_BLOCK - 1) // MOE_BLOCK * MOE_BLOCK
    pad_end = jnp.cumsum(padded)
    pad_start = pad_end - padded
    n_blk = -(-(rows + N_EXPERTS * (MOE_BLOCK - 1)) // MOE_BLOCK)
    n_rows = n_blk * MOE_BLOCK
    blk_start = jnp.arange(n_blk, dtype=jnp.int32) * MOE_BLOCK
    blk_e = jnp.minimum(jnp.sum(blk_start[:, None] >= pad_end[None, :], axis=1, dtype=jnp.int32), N_EXPERTS - 1)
    n_act = (pad_end[-1:] // MOE_BLOCK).astype(jnp.int32)
    row_off = (blk_start - pad_start[blk_e])[:, None] + jnp.arange(MOE_BLOCK, dtype=jnp.int32)[None, :]
    row_valid = row_off < counts[blk_e][:, None]
    row_flat = order[jnp.where(row_valid, start[blk_e][:, None] + row_off, 0)]
    row_tok = jnp.where(row_valid, row_flat // TOP_K, T).reshape(n_rows)
    row_gate = jnp.where(row_valid, gate.reshape(-1)[row_flat], 0.0).reshape(n_rows)
    pos = (pad_start[flat_e] + rank - start[flat_e]).reshape(T, TOP_K)

    h_pad = jnp.concatenate([h2, jnp.zeros((1, D), h2.dtype)], axis=0)
    xs = h_pad[row_tok]
    out = _moe_ffn_blocks(xs, row_gate, blk_e, n_act, *moe_w)
    y = x1
    for k in range(TOP_K):
        y = y + out[pos[:, k]]
    return y


def _trunk_layer(x, lp, w_bf, moe_w, lam, lam_init, s0, prev_row, cache=None):
    B, L, D = x.shape
    T = B * L
    x2 = x.reshape(T, D)
    tl = _tile(L, 320, BF16_ROWS)
    tm = _tile(T, 1024, BF16_ROWS) if cache is not None else _tile(L, 1024, BF16_ROWS)
    tm_small = _tile(tm, 320, BF16_ROWS)
    w_r, w_qk, w_v, w_g = w_bf

    pr = _norm_matmul(x2, lp['norm1_g'], w_r, tm_small)
    pqk = _norm_matmul(x2, lp['norm1_g'], w_qk, tm)
    pv = _norm_matmul(x2, lp['norm1_g'], w_v, tm)
    pg = _norm_matmul(x2, lp['norm1_g'], w_g, tm)

    pr3 = pr.reshape(B, L, RWKV_PROJ)
    tile_last = pr3.reshape(B, L // tl, tl, RWKV_PROJ)[:, :-1, -1]
    prev_rows = jnp.concatenate([prev_row.astype(F32), tile_last], axis=1).reshape(T // tl, 1, RWKV_PROJ)
    *scan_ops, bonus, g = _rwkv_prep(pr, prev_rows, lp, B, L, tl)
    nc = B * RWKV_HEADS
    lane_chain = _scan_lane_order(nc)
    s0c = s0.astype(F32).transpose(3, 2, 0, 1).reshape(RWKV_HEAD, RWKV_HEAD, nc)[:, :, lane_chain]
    y, s_fin_c = _rwkv_scan(scan_ops, s0c, _tile(L, SCAN_TT, 2))
    s_fin_c = s_fin_c[:, :, np.argsort(lane_chain)]
    s_fin = s_fin_c.reshape(RWKV_HEAD, RWKV_HEAD, B, RWKV_HEADS).transpose(2, 3, 1, 0)
    o_r = _rwkv_post(y, bonus, g, lp, B, L, tl)

    if cache is None:
        assert ATT_TK % ATT_TQ == 0
        lpad = -(-L // ATT_TK) * ATT_TK
        k_rows, qb, kb, vb = _qk_norm(pqk.reshape(B, L, 2 * ATT_QK), pv.reshape(B, L, ATT_DIM), lp, lpad,
                                      _tile(lpad, 512, BF16_ROWS))
        o_a = _attention(qb, kb, vb, lam, lp['subln_g'], lam_init, chunked=True, q_pos0=0, n_keys=L,
                         n_q=-(-L // ATT_TQ), tq=ATT_TQ, tk=ATT_TK)
    else:
        cache_k, cache_v = cache
        past = cache_k.shape[1]
        k_rows, qb, kb, vb = _qk_norm(pqk.reshape(B, L, 2 * ATT_QK), pv.reshape(B, L, ATT_DIM), lp, L, L)
        n_keys = past + L
        lk_pad = -(-n_keys // ATT_TK) * ATT_TK
        zk = jnp.zeros((B, lk_pad - n_keys, ATT_QK), BF16)
        k_all = jnp.concatenate([cache_k.reshape(B, past, ATT_QK).astype(BF16), kb, zk], axis=1)
        v_all = jnp.concatenate([cache_v.reshape(B, past, ATT_DIM).astype(BF16), vb, zk], axis=1)
        tq = -(-L // LANES) * LANES
        qb = jnp.concatenate([qb, jnp.zeros((B, tq - L, ATT_QK), BF16)], axis=1)
        o_a = _attention(qb, k_all, v_all, lam, lp['subln_g'], lam_init, chunked=False, q_pos0=past,
                         n_keys=n_keys, n_q=1, tq=tq, tk=ATT_TK)
    o_a = o_a[:, :L].reshape(T, ATT_DIM)

    x1, h2, logits = _merge(o_r, o_a, pg, x2, lp, tm_small)
    x_out = _moe(h2, logits[:, :N_EXPERTS], x1, moe_w)
    return (x_out.reshape(B, L, D), k_rows.reshape(B, L, ATT_HEADS, 2, QK_DIM),
            pv.reshape(B, L, ATT_HEADS, V_DIM), s_fin, pr3[:, -1:])


def kernel(x_prompt, x_sample, cache_k, cache_v, state_wkv, state_shift, meta, norm1_g, w_in, shift_mu, w0, w_up, a0, a_up, g_up, k_k, k_a, r_k, lnx_g, lnx_b, q_norm_g, k_norm_g, lam_q1, lam_k1, lam_q2, lam_k2, subln_g, w_br_r, w_br_a, w_out, norm2_g, router_w, router_b, w_gu, b_gu, w_dn, b_dn):
    depth = w_in.shape[0]
    n_prompt = x_prompt.shape[0]
    meta_rows = jnp.broadcast_to(meta.astype(x_prompt.dtype)[None], (n_prompt, N_META, D_MODEL))
    xp = jnp.concatenate([meta_rows, x_prompt], axis=1)
    xs = x_sample
    st_p, st_s = [], []
    for l in range(depth):
        lp = dict(norm1_g=norm1_g[l], shift_mu=shift_mu[l], w0=w0[l], w_up=w_up[l],
                  a0=a0[l], a_up=a_up[l], g_up=g_up[l], k_k=k_k[l], k_a=k_a[l], r_k=r_k[l].reshape(-1),
                  lnx_g=lnx_g[l], lnx_b=lnx_b[l], q_norm_g=q_norm_g[l], k_norm_g=k_norm_g[l],
                  subln_g=subln_g[l], w_br_r=w_br_r[l], w_br_a=w_br_a[l], w_out=w_out[l],
                  norm2_g=norm2_g[l], router_w=router_w[l], router_b=router_b[l])
        wl = w_in[l].astype(BF16)
        c0, c1, c2 = RWKV_PROJ, RWKV_PROJ + 2 * ATT_QK, RWKV_PROJ + 2 * ATT_QK + ATT_DIM
        w_bf = (wl[:, :c0], wl[:, c0:c1], wl[:, c1:c2], wl[:, c2:])
        moe_w = (w_gu[l].astype(F32), b_gu[l].astype(F32), w_dn[l].astype(F32), b_dn[l].astype(F32))
        lam_init = 0.8 - 0.6 * math.exp(-0.3 * l)
        lam = (jnp.exp(jnp.sum(lam_q1[l].astype(F32) * lam_k1[l].astype(F32)))
               - jnp.exp(jnp.sum(lam_q2[l].astype(F32) * lam_k2[l].astype(F32))) + lam_init)
        s0 = jnp.zeros((n_prompt, RWKV_HEADS, RWKV_HEAD, RWKV_HEAD), F32)
        r0 = jnp.zeros((n_prompt, 1, RWKV_PROJ), F32)
        xp, kp, vp, sp, rp = _trunk_layer(xp, lp, w_bf, moe_w, lam, lam_init, s0, r0)
        xs, ks, vs, ss, rs = _trunk_layer(xs, lp, w_bf, moe_w, lam, lam_init, state_wkv[l], state_shift[l],
                                          cache=(cache_k[l], cache_v[l]))
        st_p.append((kp, vp, sp, rp))
        st_s.append((ks, vs, ss, rs))
    stack = lambda st, j: jnp.stack([t[j] for t in st])
    return (xp[:, N_META:], xs,
            stack(st_p, 0), stack(st_p, 1), stack(st_p, 2), stack(st_p, 3),
            stack(st_s, 0), stack(st_s, 1), stack(st_s, 2), stack(st_s, 3))
```

```python
import functools
import math

import jax
import jax.numpy as jnp
import numpy as np
from jax import lax
from jax.experimental import pallas as pl
from jax.experimental.pallas import tpu as pltpu

F32 = jnp.float32
BF16 = jnp.bfloat16

D_MODEL = 1024
CHUNK = 64
CHUNK_SHIFT = CHUNK.bit_length() - 1
assert 1 << CHUNK_SHIFT == CHUNK
N_META = 16
RWKV_HEAD = 64
RWKV_HEADS = D_MODEL // RWKV_HEAD
RWKV_DIM = RWKV_HEADS * RWKV_HEAD
DECAY_LORA = 64
AAA_LORA = 64
GATE_LORA = 128
RWKV_PROJ = 3 * RWKV_DIM + DECAY_LORA + AAA_LORA + GATE_LORA
LNX_EPS = 64e-5
ATT_HEADS = 8
QK_DIM = 64
V_DIM = 2 * QK_DIM
ATT_QK = ATT_HEADS * 2 * QK_DIM
ATT_DIM = ATT_HEADS * V_DIM
N_EXPERTS = 32
TOP_K = 4
SWIGLU_ALPHA = 1.702
SWIGLU_LIMIT = 7.0
RMS_EPS = 1e-6
NEG_INF = -1e30
LOG2E = math.log2(math.e)
Q_SCALE = QK_DIM ** -0.5 * LOG2E


def _bf16_terms(x, n):
    terms = []
    for _ in range(n):
        t = float(np.float32(x).astype(jnp.bfloat16))
        terms.append(t)
        x -= t
    return tuple(terms)


LOG2E_BF16_TERMS = _bf16_terms(LOG2E, 3)

LANES = 128
SUBLANES = 8
BF16_ROWS = 16
BF16_EXACT_INT = 256
MXU_DIM = 256
VMEM_LIMIT = 56 * 1024 * 1024

GROUP_BLOCK = MXU_DIM
MOE_BLOCK = 512
ATT_TQ = 256
ATT_TK = 512
ATT_SUB = 256
SCAN_TT = 48
SCAN_ACCUMULATORS = 4


def _cparams(*sem):
    return pltpu.CompilerParams(dimension_semantics=sem, vmem_limit_bytes=VMEM_LIMIT)


def _tile(n, target, mult):
    best = None
    for t in range(mult, min(n, target) + 1, mult):
        if n % t == 0:
            best = t
    return best if best is not None else n


def _group_ones():
    g = np.arange(GROUP_BLOCK) // RWKV_HEAD
    return jnp.asarray((g[:, None] == g[None, :]).astype(np.float32), BF16)


def _gsum(x, ones_ref):
    hi = x.astype(BF16)
    lo = (x - hi.astype(F32)).astype(BF16)
    ones = ones_ref[...]
    outs = []
    for c in range(x.shape[1] // GROUP_BLOCK):
        sl = slice(c * GROUP_BLOCK, (c + 1) * GROUP_BLOCK)
        outs.append(jnp.dot(hi[:, sl], ones, preferred_element_type=F32)
                    + jnp.dot(lo[:, sl], ones, preferred_element_type=F32))
    return jnp.concatenate(outs, axis=1)


def _norm_matmul_kernel(x_ref, g_ref, w_ref, o_ref):
    x = x_ref[...]
    h = x * lax.rsqrt(jnp.mean(x * x, axis=-1, keepdims=True) + RMS_EPS) * g_ref[...]
    o_ref[...] = jnp.dot(h.astype(BF16), w_ref[...], preferred_element_type=F32)


def _norm_matmul(x, g, w, tm):
    T, D = x.shape
    N = w.shape[1]
    return pl.pallas_call(
        _norm_matmul_kernel,
        grid=(T // tm,),
        in_specs=[pl.BlockSpec((tm, D), lambda i: (i, 0)),
                  pl.BlockSpec((1, D), lambda i: (0, 0)),
                  pl.BlockSpec((D, N), lambda i: (0, 0))],
        out_specs=pl.BlockSpec((tm, N), lambda i: (i, 0)),
        out_shape=jax.ShapeDtypeStruct((T, N), F32),
        compiler_params=_cparams("parallel"),
        name="norm_matmul",
    )(x, g.reshape(1, D), w)


def _rwkv_prep_kernel(pr_ref, prev_ref, mu_ref, w0_ref, wup_ref, a0_ref, aup_ref, gup_ref,
                      kk_ref, ka_ref, rk_ref, ones_ref,
                      kn_o, d_o, b_o, k_o, v_o, r_o, bonus_o, g_o):
    pr = pr_ref[...]
    row = lax.broadcasted_iota(jnp.int32, pr.shape, 0)
    prev = jnp.where(row == 0, prev_ref[...], pltpu.roll(pr, 1, axis=0))
    xs = pr + (prev - pr) * mu_ref[...]
    r = xs[:, :RWKV_DIM]
    k = xs[:, RWKV_DIM:2 * RWKV_DIM]
    v = xs[:, 2 * RWKV_DIM:3 * RWKV_DIM]
    lora_in = xs[:, 3 * RWKV_DIM:3 * RWKV_DIM + DECAY_LORA + AAA_LORA]
    gd = xs[:, 3 * RWKV_DIM + DECAY_LORA + AAA_LORA:]
    wl = jnp.dot(jnp.tanh(lora_in).astype(BF16), wup_ref[...], preferred_element_type=F32)
    al = jnp.dot(lora_in.astype(BF16), aup_ref[...], preferred_element_type=F32)
    z = -(w0_ref[...] + wl)
    w = -(jnp.maximum(z, 0.0) + jnp.log1p(jnp.exp(-jnp.abs(z)))) - 0.5
    decay = jnp.exp(-jnp.exp(w))
    a = jax.nn.sigmoid(a0_ref[...] + al)
    g = jnp.dot(jax.nn.sigmoid(gd).astype(BF16), gup_ref[...], preferred_element_type=F32)
    kk = k * kk_ref[...]
    nrm = jnp.sqrt(_gsum(kk * kk, ones_ref))
    kk = kk / jnp.maximum(nrm, 1e-12)
    k = k * (1.0 + (a - 1.0) * ka_ref[...])
    slab = lambda val: val.reshape(val.shape[0], RWKV_DIM // LANES, LANES)
    kn_o[...] = slab(-kk)
    d_o[...] = slab(decay)
    b_o[...] = slab(kk * a)
    k_o[...] = slab(k)
    v_o[...] = slab(v)
    r_o[...] = slab(r)
    bonus_o[...] = _gsum(r * k * rk_ref[...], ones_ref) * v
    g_o[...] = g


def _rwkv_prep(pr, prev_rows, lp, B, L, tl):
    nt = L // tl
    rows_per_token = RWKV_DIM // LANES
    row = lambda a: a.reshape(1, -1).astype(F32)
    zpad = jnp.zeros((DECAY_LORA, RWKV_DIM), F32)
    wup = jnp.concatenate([lp['w_up'], zpad], axis=0).astype(BF16)
    aup = jnp.concatenate([zpad, lp['a_up']], axis=0).astype(BF16)
    vec = lambda n: pl.BlockSpec((1, n), lambda b, i: (0, 0))
    mat = lambda m, n: pl.BlockSpec((m, n), lambda b, i: (0, 0))
    tok = pl.BlockSpec((tl, RWKV_DIM), lambda b, i: (b * nt + i, 0))
    time_major = pl.BlockSpec((tl, rows_per_token, LANES), lambda b, i: (i, b, 0))
    return pl.pallas_call(
        _rwkv_prep_kernel,
        grid=(B, nt),
        in_specs=[pl.BlockSpec((tl, RWKV_PROJ), lambda b, i: (b * nt + i, 0)),
                  pl.BlockSpec((None, 1, RWKV_PROJ), lambda b, i: (b * nt + i, 0, 0)),
                  vec(RWKV_PROJ), vec(RWKV_DIM), mat(2 * DECAY_LORA, RWKV_DIM), vec(RWKV_DIM),
                  mat(2 * AAA_LORA, RWKV_DIM), mat(GATE_LORA, RWKV_DIM),
                  vec(RWKV_DIM), vec(RWKV_DIM), vec(RWKV_DIM), mat(GROUP_BLOCK, GROUP_BLOCK)],
        out_specs=[time_major] * 6 + [tok] * 2,
        out_shape=[jax.ShapeDtypeStruct((L, B * rows_per_token, LANES), F32)] * 6
                  + [jax.ShapeDtypeStruct((B * L, RWKV_DIM), F32)] * 2,
        compiler_params=_cparams("parallel", "parallel"),
        name="rwkv_prep",
    )(pr, prev_rows, row(lp['shift_mu']), row(lp['w0']), wup, row(lp['a0']), aup,
      lp['g_up'].astype(BF16), row(lp['k_k']), row(lp['k_a']), row(lp['r_k']), _group_ones())


def _rwkv_scan_kernel(kn_ref, d_ref, b_ref, k_ref, v_ref, r_ref, s0_ref, y_ref, sfin_ref,
                      s_sc, opa_sc, opb_sc):
    tb = pl.program_id(1)

    @pl.when(tb == 0)
    def _():
        s_sc[...] = s0_ref[...]

    tt = kn_ref.shape[0]
    refs = (kn_ref, d_ref, b_ref, k_ref, v_ref, r_ref)

    def stage(t, op_sc):
        t = jnp.minimum(t, tt - 1)
        for i, ref in enumerate(refs):
            mt = ref[t].T
            op_sc[i] = jnp.concatenate([mt[:RWKV_HEAD], mt[RWKV_HEAD:]], axis=1)

    def tree_sum(terms):
        while len(terms) > 1:
            terms = [a + b for a, b in zip(terms[::2], terms[1::2])]
        return terms[0]

    class Sums:
        def __init__(self):
            self.partial = [None] * SCAN_ACCUMULATORS

        def add(self, k, term):
            a = k % SCAN_ACCUMULATORS
            self.partial[a] = term if self.partial[a] is None else self.partial[a] + term

        def total(self):
            return tree_sum(self.partial)

    def state_dot(op_sc):
        acc = Sums()
        for k in range(RWKV_HEAD):
            acc.add(k, s_sc[k] * op_sc[0, k:k + 1, :])
        return acc.total()

    def step(t, op_sc, next_sc, sa):
        row = lambda i, k: op_sc[i, k:k + 1, :]
        vv = op_sc[4]
        y_acc, sa_acc = Sums(), Sums()
        for k in range(RWKV_HEAD):
            sn = s_sc[k] * row(1, k) + sa * row(2, k) + vv * row(3, k)
            s_sc[k] = sn
            y_acc.add(k, sn * row(5, k))
            sa_acc.add(k, sn * next_sc[0, k:k + 1, :])
        y = y_acc.total()
        y_ref[t] = jnp.concatenate([y[:, :RWKV_HEAD], y[:, RWKV_HEAD:]], axis=0).T
        return sa_acc.total()

    stage(0, opa_sc)

    def pair(i, sa):
        t0 = 2 * i
        stage(t0 + 1, opb_sc)
        sa = step(t0, opa_sc, opb_sc, sa)
        stage(t0 + 2, opa_sc)
        return step(t0 + 1, opb_sc, opa_sc, sa)

    lax.fori_loop(0, tt // 2, pair, state_dot(opa_sc))

    @pl.when(tb == pl.num_programs(1) - 1)
    def _():
        sfin_ref[...] = s_sc[...]


def _scan_lane_order(nc):
    lane = np.arange(nc)
    return (lane // LANES) * LANES + 2 * (lane % RWKV_HEAD) + (lane % LANES) // RWKV_HEAD


def _rwkv_scan(ops, s0, tt):
    L, pairs, _ = ops[0].shape
    n = RWKV_HEAD
    nc = 2 * pairs
    assert tt % 2 == 0 and L % tt == 0 and nc % LANES == 0
    tspec = pl.BlockSpec((tt, n, LANES), lambda c, t: (t, c, 0))
    sspec = pl.BlockSpec((n, n, LANES), lambda c, t: (0, 0, c))
    return pl.pallas_call(
        _rwkv_scan_kernel,
        grid=(nc // LANES, L // tt),
        in_specs=[tspec] * 6 + [sspec],
        out_specs=[tspec, sspec],
        out_shape=[jax.ShapeDtypeStruct((L, pairs, LANES), F32), jax.ShapeDtypeStruct((n, n, nc), F32)],
        scratch_shapes=[pltpu.VMEM((n, n, LANES), F32), pltpu.VMEM((6, n, LANES), F32),
                        pltpu.VMEM((6, n, LANES), F32)],
        compiler_params=_cparams("parallel", "arbitrary"),
        name="rwkv_scan",
    )(*ops, s0)


def _rwkv_post_kernel(y_ref, bonus_ref, g_ref, lg_ref, lb_ref, ones_ref, o_ref):
    y = jnp.concatenate([y_ref[:, j, :] for j in range(y_ref.shape[1])], axis=1)
    inv_n = 1.0 / RWKV_HEAD
    mean = _gsum(y, ones_ref) * inv_n
    yc = y - mean
    var = _gsum(yc * yc, ones_ref) * inv_n
    yn = yc * lax.rsqrt(var + LNX_EPS) * lg_ref[...] + lb_ref[...]
    o_ref[...] = ((yn + bonus_ref[...]) * g_ref[...]).astype(BF16)


def _rwkv_post(y, bonus, g, lp, B, L, tl):
    nt = L // tl
    tok = pl.BlockSpec((tl, RWKV_DIM), lambda b, i: (b * nt + i, 0))
    vec = pl.BlockSpec((1, RWKV_DIM), lambda b, i: (0, 0))
    return pl.pallas_call(
        _rwkv_post_kernel,
        grid=(B, nt),
        in_specs=[pl.BlockSpec((tl, RWKV_DIM // LANES, LANES), lambda b, i: (i, b, 0)), tok, tok, vec, vec,
                  pl.BlockSpec((GROUP_BLOCK, GROUP_BLOCK), lambda b, i: (0, 0))],
        out_specs=tok,
        out_shape=jax.ShapeDtypeStruct((B * L, RWKV_DIM), BF16),
        compiler_params=_cparams("parallel", "parallel"),
        name="rwkv_post",
    )(y, bonus, g, lp['lnx_g'].reshape(1, -1), lp['lnx_b'].reshape(1, -1), _group_ones())


def _qk_norm_kernel(qk_ref, v_ref, qg_ref, kg_ref, ones_ref, kout_ref, qb_ref, kb_ref, vb_ref, *, length):
    tl = qk_ref.shape[0]
    valid = (pl.program_id(1) * tl + lax.broadcasted_iota(jnp.int32, (tl, 1), 0)) < length
    q = qk_ref[:, :ATT_QK]
    k = qk_ref[:, ATT_QK:]
    inv_n = 1.0 / QK_DIM
    qn = q * lax.rsqrt(_gsum(q * q, ones_ref) * inv_n + RMS_EPS) * qg_ref[...]
    kn = k * lax.rsqrt(_gsum(k * k, ones_ref) * inv_n + RMS_EPS) * kg_ref[...]
    kout_ref[...] = kn
    qb_ref[...] = jnp.where(valid, qn * Q_SCALE, 0.0).astype(BF16)
    kb_ref[...] = jnp.where(valid, kn, 0.0).astype(BF16)
    vb_ref[...] = jnp.where(valid, v_ref[...], 0.0).astype(BF16)


def _qk_norm(pqk, pv, lp, lpad, tl):
    B, L, _ = pqk.shape
    tile_g = lambda g: jnp.tile(g.astype(F32), ATT_QK // QK_DIM).reshape(1, ATT_QK)
    blk = lambda n: pl.BlockSpec((None, tl, n), lambda b, i: (b, i, 0))
    vec = pl.BlockSpec((1, ATT_QK), lambda b, i: (0, 0))
    bshape = jax.ShapeDtypeStruct((B, lpad, ATT_QK), BF16)
    return pl.pallas_call(
        functools.partial(_qk_norm_kernel, length=L),
        grid=(B, lpad // tl),
        in_specs=[blk(2 * ATT_QK), blk(ATT_DIM), vec, vec,
                  pl.BlockSpec((GROUP_BLOCK, GROUP_BLOCK), lambda b, i: (0, 0))],
        out_specs=[blk(ATT_QK)] * 4,
        out_shape=[jax.ShapeDtypeStruct((B, L, ATT_QK), F32), bshape, bshape, bshape],
        compiler_params=_cparams("parallel", "parallel"),
        name="qk_norm",
    )(pqk, pv, tile_g(lp['q_norm_g']), tile_g(lp['k_norm_g']), _group_ones())


def _attn_kernel(slope_ref, lam_ref, q_ref, k_ref, kpos_ref, v_ref, g_ref, o_ref,
                 qt_sc, vt_sc, pen_sc, sa_sc, sb_sc, m_sc, a_sc,
                 *, chunked, q_pos0, n_keys, out_scale):
    tq = q_ref.shape[0]
    n_tiles, _, tk = vt_sc.shape
    h = pl.program_id(1)
    i = pl.program_id(2)

    @pl.when(i == 0)
    def _():
        ones_rows = (lax.broadcasted_iota(jnp.int32, (BF16_ROWS, tk), 0) == 0).astype(F32).astype(BF16)

        def transpose_tile(j, carry):
            k0 = pl.multiple_of(j * tk, tk)
            vt_sc[j, :V_DIM, :] = v_ref[pl.ds(k0, tk), :].astype(F32).T.astype(BF16)
            vt_sc[j, V_DIM:, :] = ones_rows
            return carry
        lax.fori_loop(0, n_tiles, transpose_tile, 0)

    qt = q_ref[...].astype(F32).T
    sub = lax.broadcasted_iota(jnp.int32, qt.shape, 0)
    slope = slope_ref[h]
    coef = jnp.zeros(qt.shape, F32)
    for r, c in enumerate([w * c for w in (CHUNK, 1) for c in LOG2E_BF16_TERMS]):
        coef = jnp.where(sub == r, slope * c, coef)
    coef = coef.astype(BF16)
    for mi, keep in enumerate((sub < QK_DIM, sub >= QK_DIM)):
        qt_sc[mi, :2 * QK_DIM, :] = jnp.where(keep, qt, 0.0).astype(BF16)
        qt_sc[mi, 2 * QK_DIM:, :] = coef
    q_first = q_pos0 + i * tq
    qpos = q_first + lax.broadcasted_iota(jnp.int32, (1, tq), 1)
    if chunked:
        chunk_end = lambda p: N_META + CHUNK + (((p - N_META) >> CHUNK_SHIFT) << CHUNK_SHIFT)
        vis = jnp.where(qpos < N_META, N_META, chunk_end(qpos))
        vis_last = chunk_end(q_first + tq - 1)
        n_kt = (jnp.minimum(vis_last, n_keys) + tk - 1) // tk
    else:
        vis = jnp.full((1, tq), n_keys, jnp.int32)
        n_kt = (n_keys + tk - 1) // tk
    n_past = jnp.minimum((q_first + 1) // tk, n_kt)

    m_sc[...] = jnp.full(m_sc.shape, -jnp.inf, F32)
    a_sc[...] = jnp.zeros(a_sc.shape, F32)

    def accumulate(mi, s, vtj):
        m_old = m_sc[mi]
        m_new = jnp.maximum(m_old, jnp.max(s, axis=0, keepdims=True))
        p = jnp.exp2(s - m_new)
        a_sc[mi] = jnp.exp2(m_old - m_new) * a_sc[mi] + jnp.dot(vtj, p.astype(BF16),
                                                                preferred_element_type=F32)
        m_sc[mi] = m_new

    def scores_into(s_ref, j):
        k0 = pl.multiple_of(jnp.minimum(j, n_tiles - 1) * tk, tk)
        kj = jnp.concatenate([k_ref[pl.ds(k0, tk), :], kpos_ref[pl.ds(k0, tk), :]], axis=1)
        for mi in range(2):
            s_ref[mi] = jnp.dot(kj, qt_sc[mi], preferred_element_type=F32)

    def consume(s_ref, j, pen_ref=None):
        for k0 in range(0, tk, ATT_SUB):
            for mi in range(2):
                s = s_ref[mi, k0:k0 + ATT_SUB, :]
                if pen_ref is not None:
                    s = s - pen_ref[k0:k0 + ATT_SUB, :]
                accumulate(mi, s, vt_sc[j, :, k0:k0 + ATT_SUB])

    n_pairs = n_past // 2

    @pl.when(n_pairs > 0)
    def _():
        scores_into(sa_sc, 0)

    def pair_body(jj, carry):
        j0 = 2 * jj
        scores_into(sb_sc, j0 + 1)
        consume(sa_sc, j0)
        scores_into(sa_sc, j0 + 2)
        consume(sb_sc, j0 + 1)
        return carry

    lax.fori_loop(0, n_pairs, pair_body, 0)

    def past_single(j, carry):
        scores_into(sa_sc, j)
        consume(sa_sc, j)
        return carry

    lax.fori_loop(2 * n_pairs, n_past, past_single, 0)

    def edge_body(j, carry):
        kpos = j * tk + lax.broadcasted_iota(jnp.int32, (tk, 1), 0)
        ahead = jnp.maximum(kpos - qpos, 0).astype(F32)
        pen_sc[...] = jnp.where(kpos < vis, (2.0 * LOG2E * slope) * ahead, -NEG_INF)
        scores_into(sa_sc, j)
        consume(sa_sc, j, pen_sc)
        return carry

    lax.fori_loop(n_past, n_kt, edge_body, 0)

    num = [a_sc[mi, :V_DIM, :] / a_sc[mi, V_DIM:V_DIM + 1, :] for mi in range(2)]
    o = num[0] - lam_ref[0] * num[1]
    o = o * lax.rsqrt(jnp.mean(o * o, axis=0, keepdims=True) + RMS_EPS) * g_ref[...] * out_scale
    o_ref[...] = o.T.astype(BF16)


def _attention(qb, kb, vb, lam, subln_g, lam_init, *, chunked, q_pos0, n_keys, n_q, tq, tk):
    B = qb.shape[0]
    lk = kb.shape[1]
    slopes = jnp.asarray([2.0 ** (-8.0 * (i + 1) / ATT_HEADS) for i in range(ATT_HEADS)], F32)
    gain = jnp.broadcast_to(subln_g.astype(F32).reshape(V_DIM, 1), (V_DIM, tq))
    assert lk // CHUNK <= BF16_EXACT_INT
    kpos = np.arange(lk)
    digits = np.zeros((lk, V_DIM), np.float32)
    digits[:, 0:3] = (kpos // CHUNK)[:, None]
    digits[:, 3:6] = (kpos % CHUNK)[:, None]
    kpos_digits = jnp.asarray(digits, BF16)
    smem = pl.BlockSpec(memory_space=pltpu.SMEM)
    qspec = pl.BlockSpec((None, tq, V_DIM), lambda b, h, i: (b, i, h))
    kspec = pl.BlockSpec((None, lk, V_DIM), lambda b, h, i: (b, 0, h))
    const = lambda shape: pl.BlockSpec(shape, lambda b, h, i: (0, 0))
    acc_rows = V_DIM + BF16_ROWS
    return pl.pallas_call(
        functools.partial(_attn_kernel, chunked=chunked, q_pos0=q_pos0, n_keys=n_keys,
                          out_scale=1.0 - lam_init),
        grid=(B, ATT_HEADS, n_q),
        in_specs=[smem, smem, qspec, kspec, const((lk, V_DIM)), kspec, const((V_DIM, tq))],
        out_specs=qspec,
        out_shape=jax.ShapeDtypeStruct((B, n_q * tq, ATT_DIM), BF16),
        scratch_shapes=[pltpu.VMEM((2, 2 * V_DIM, tq), BF16), pltpu.VMEM((lk // tk, acc_rows, tk), BF16),
                        pltpu.VMEM((tk, tq), F32),
                        pltpu.VMEM((2, tk, tq), F32), pltpu.VMEM((2, tk, tq), F32),
                        pltpu.VMEM((2, 1, tq), F32),
                        pltpu.VMEM((2, acc_rows, tq), F32)],
        compiler_params=_cparams("parallel", "parallel", "arbitrary"),
        name="diff_attention",
    )(slopes, lam.reshape(1).astype(F32), qb, kb, kpos_digits, vb, gain)


def _merge_kernel(or_ref, oa_ref, pg_ref, x_ref, wr_ref, wa_ref, wo_ref, g2_ref, rw_ref, rb_ref,
                  x1_ref, h2_ref, lg_ref):
    gates = jax.nn.sigmoid(pg_ref[...])
    br = jnp.dot(or_ref[...], wr_ref[...], preferred_element_type=F32)
    ba = jnp.dot(oa_ref[...], wa_ref[...], preferred_element_type=F32)
    merged = gates[:, :D_MODEL] * br + gates[:, D_MODEL:] * ba
    x1 = x_ref[...] + jnp.dot(merged.astype(BF16), wo_ref[...], preferred_element_type=F32)
    h2 = x1 * lax.rsqrt(jnp.mean(x1 * x1, axis=-1, keepdims=True) + RMS_EPS) * g2_ref[...]
    x1_ref[...] = x1
    h2_ref[...] = h2.astype(BF16)
    h_hi = h2.astype(BF16)
    h_lo = (h2 - h_hi.astype(F32)).astype(BF16)
    w_hi, w_lo = rw_ref[0], rw_ref[1]
    lg_ref[...] = (jnp.dot(h_hi, w_hi, preferred_element_type=F32)
                   + jnp.dot(h_hi, w_lo, preferred_element_type=F32)
                   + jnp.dot(h_lo, w_hi, preferred_element_type=F32)) + rb_ref[...]


def _merge(o_r, o_a, pg, x, lp, tm):
    T = x.shape[0]
    rw = jnp.zeros((D_MODEL, LANES), F32).at[:, :N_EXPERTS].set(lp['router_w'].astype(F32))
    rw_hi = rw.astype(BF16)
    rw = jnp.stack([rw_hi, (rw - rw_hi.astype(F32)).astype(BF16)])
    rb = jnp.zeros((1, LANES), F32).at[0, :N_EXPERTS].set(lp['router_b'].astype(F32))
    tok = lambda n: pl.BlockSpec((tm, n), lambda i: (i, 0))
    mat = lambda m, n: pl.BlockSpec((m, n), lambda i: (0, 0))
    return pl.pallas_call(
        _merge_kernel,
        grid=(T // tm,),
        in_specs=[tok(RWKV_DIM), tok(ATT_DIM), tok(2 * D_MODEL), tok(D_MODEL),
                  mat(RWKV_DIM, D_MODEL), mat(ATT_DIM, D_MODEL), mat(D_MODEL, D_MODEL),
                  mat(1, D_MODEL), pl.BlockSpec((2, D_MODEL, LANES), lambda i: (0, 0, 0)), mat(1, LANES)],
        out_specs=[tok(D_MODEL), tok(D_MODEL), tok(LANES)],
        out_shape=[jax.ShapeDtypeStruct((T, D_MODEL), F32), jax.ShapeDtypeStruct((T, D_MODEL), BF16),
                   jax.ShapeDtypeStruct((T, LANES), F32)],
        compiler_params=_cparams("parallel"),
        name="merge_router",
    )(o_r, o_a, pg, x, lp['w_br_r'].astype(BF16), lp['w_br_a'].astype(BF16), lp['w_out'].astype(BF16),
      lp['norm2_g'].reshape(1, -1), rw, rb)


def _moe_kernel(be_ref, nact_ref, x_ref, gate_ref, wgu_ref, bgu_ref, wdn_ref, bdn_ref, o_ref,
                wgu_sc, wdn_sc):
    i = pl.program_id(0)
    active = i < nact_ref[0]
    new_expert = jnp.logical_or(i == 0, be_ref[i] != be_ref[jnp.maximum(i - 1, 0)])

    @pl.when(jnp.logical_and(active, new_expert))
    def _():
        for dst, src in ((wgu_sc, wgu_ref), (wdn_sc, wdn_ref)):
            for r0 in range(0, src.shape[0], MXU_DIM):
                dst[r0:r0 + MXU_DIM, :] = src[r0:r0 + MXU_DIM, :].astype(BF16)

    @pl.when(active)
    def _():
        gu = jnp.dot(x_ref[...], wgu_sc[...], preferred_element_type=F32) + bgu_ref[...]
        d_ff = gu.shape[1] // 2
        x_glu = jnp.minimum(gu[:, :d_ff], SWIGLU_LIMIT)
        x_lin = jnp.clip(gu[:, d_ff:], -SWIGLU_LIMIT, SWIGLU_LIMIT)
        act = x_glu * jax.nn.sigmoid(SWIGLU_ALPHA * x_glu) * (x_lin + 1.0)
        out = jnp.dot(act.astype(BF16), wdn_sc[...], preferred_element_type=F32) + bdn_ref[...]
        o_ref[...] = out * gate_ref[...]

    @pl.when(i >= nact_ref[0])
    def _():
        o_ref[...] = jnp.zeros_like(o_ref)


def _moe_ffn_blocks(xs, row_gate, blk_e, n_act, w_gu, b_gu, w_dn, b_dn):
    n_rows, D = xs.shape
    n_blk = n_rows // MOE_BLOCK
    d_ff2 = w_gu.shape[2]
    grid_spec = pltpu.PrefetchScalarGridSpec(
        num_scalar_prefetch=2,
        grid=(n_blk,),
        in_specs=[pl.BlockSpec((MOE_BLOCK, D), lambda i, be, na: (i, 0)),
                  pl.BlockSpec((MOE_BLOCK, 1), lambda i, be, na: (i, 0)),
                  pl.BlockSpec((None, D, d_ff2), lambda i, be, na: (be[i], 0, 0)),
                  pl.BlockSpec((None, 1, d_ff2), lambda i, be, na: (be[i], 0, 0)),
                  pl.BlockSpec((None, d_ff2 // 2, D), lambda i, be, na: (be[i], 0, 0)),
                  pl.BlockSpec((None, 1, D), lambda i, be, na: (be[i], 0, 0))],
        out_specs=pl.BlockSpec((MOE_BLOCK, D), lambda i, be, na: (i, 0)),
        scratch_shapes=[pltpu.VMEM((D, d_ff2), BF16), pltpu.VMEM((d_ff2 // 2, D), BF16)],
    )
    return pl.pallas_call(
        _moe_kernel,
        grid_spec=grid_spec,
        out_shape=jax.ShapeDtypeStruct((n_rows, D), F32),
        compiler_params=_cparams("arbitrary"),
        name="moe_ffn",
    )(blk_e, n_act, xs, row_gate.reshape(n_rows, 1), w_gu, b_gu.reshape(N_EXPERTS, 1, d_ff2),
      w_dn, b_dn.reshape(N_EXPERTS, 1, D))


def _moe_dispatch(h2, logits):
    T, D = h2.shape
    top_v, top_e = lax.top_k(logits, TOP_K)
    gate = jax.nn.softmax(top_v, axis=-1)
    rows = T * TOP_K
    flat_e = top_e.reshape(-1).astype(jnp.int32)
    e_sorted, order = lax.sort((flat_e, jnp.arange(rows, dtype=jnp.int32)), num_keys=1, is_stable=True)
    rank = jnp.argsort(order).astype(jnp.int32)
    bounds = jnp.searchsorted(e_sorted, jnp.arange(N_EXPERTS + 1, dtype=jnp.int32)).astype(jnp.int32)
    start, counts = bounds[:-1], bounds[1:] - bounds[:-1]
    padded = (counts + MOE_BLOCK - 1) // MOE_BLOCK * MOE_BLOCK
    pad_end = jnp.cumsum(padded)
    pad_start = pad_end - padded
    n_blk = -(-(rows + N_EXPERTS * (MOE_BLOCK - 1)) // MOE_BLOCK)
    n_rows = n_blk * MOE_BLOCK
    blk_start = jnp.arange(n_blk, dtype=jnp.int32) * MOE_BLOCK
    blk_e = jnp.minimum(jnp.sum(blk_start[:, None] >= pad_end[None, :], axis=1, dtype=jnp.int32), N_EXPERTS - 1)
    n_act = (pad_end[-1:] // MOE_BLOCK).astype(jnp.int32)
    row_off = (blk_start - pad_start[blk_e])[:, None] + jnp.arange(MOE_BLOCK, dtype=jnp.int32)[None, :]
    row_valid = row_off < counts[blk_e][:, None]
    row_flat = order[jnp.where(row_valid, start[blk_e][:, None] + row_off, 0)]
    row_tok = jnp.where(row_valid, row_flat // TOP_K, T).reshape(n_rows)
    row_gate = jnp.where(row_valid, gate.reshape(-1)[row_flat], 0.0).reshape(n_rows)
    pos = (pad_start[flat_e] + rank - start[flat_e]).reshape(T, TOP_K)

    h_pad = jnp.concatenate([h2, jnp.zeros((1, D), h2.dtype)], axis=0)
    return h_pad[row_tok], row_gate, blk_e, n_act, pos


def _moe_finish(x1, dispatch, moe_w):
    xs, row_gate, blk_e, n_act, pos = dispatch
    out = _moe_ffn_blocks(xs, row_gate, blk_e, n_act, *moe_w)
    y = x1
    for k in range(TOP_K):
        y = y + out[pos[:, k]]
    return y


def _trunk_layer(x, lp, w_bf, moe_w, lam, lam_init, s0, prev_row, cache=None):
    B, L, D = x.shape
    T = B * L
    x2 = x.reshape(T, D)
    tl = _tile(L, 320, BF16_ROWS)
    tm = _tile(T, 1024, BF16_ROWS) if cache is not None else _tile(L, 1024, BF16_ROWS)
    tm_small = _tile(tm, 320, BF16_ROWS)
    w_r, w_qk, w_v, w_g = w_bf

    pr = _norm_matmul(x2, lp['norm1_g'], w_r, tm_small)
    pqk = _norm_matmul(x2, lp['norm1_g'], w_qk, tm)
    pv = _norm_matmul(x2, lp['norm1_g'], w_v, tm)
    pg = _norm_matmul(x2, lp['norm1_g'], w_g, tm)

    pr3 = pr.reshape(B, L, RWKV_PROJ)
    tile_last = pr3.reshape(B, L // tl, tl, RWKV_PROJ)[:, :-1, -1]
    prev_rows = jnp.concatenate([prev_row.astype(F32), tile_last], axis=1).reshape(T // tl, 1, RWKV_PROJ)
    *scan_ops, bonus, g = _rwkv_prep(pr, prev_rows, lp, B, L, tl)
    nc = B * RWKV_HEADS
    lane_chain = _scan_lane_order(nc)
    s0c = s0.astype(F32).transpose(3, 2, 0, 1).reshape(RWKV_HEAD, RWKV_HEAD, nc)[:, :, lane_chain]
    y, s_fin_c = _rwkv_scan(scan_ops, s0c, _tile(L, SCAN_TT, 2))
    s_fin_c = s_fin_c[:, :, np.argsort(lane_chain)]
    s_fin = s_fin_c.reshape(RWKV_HEAD, RWKV_HEAD, B, RWKV_HEADS).transpose(2, 3, 1, 0)
    o_r = _rwkv_post(y, bonus, g, lp, B, L, tl)

    if cache is None:
        assert ATT_TK % ATT_TQ == 0
        lpad = -(-L // ATT_TK) * ATT_TK
        k_rows, qb, kb, vb = _qk_norm(pqk.reshape(B, L, 2 * ATT_QK), pv.reshape(B, L, ATT_DIM), lp, lpad,
                                      _tile(lpad, 512, BF16_ROWS))
        o_a = _attention(qb, kb, vb, lam, lp['subln_g'], lam_init, chunked=True, q_pos0=0, n_keys=L,
                         n_q=-(-L // ATT_TQ), tq=ATT_TQ, tk=ATT_TK)
    else:
        cache_k, cache_v = cache
        past = cache_k.shape[1]
        k_rows, qb, kb, vb = _qk_norm(pqk.reshape(B, L, 2 * ATT_QK), pv.reshape(B, L, ATT_DIM), lp, L, L)
        n_keys = past + L
        lk_pad = -(-n_keys // ATT_TK) * ATT_TK
        zk = jnp.zeros((B, lk_pad - n_keys, ATT_QK), BF16)
        k_all = jnp.concatenate([cache_k.reshape(B, past, ATT_QK).astype(BF16), kb, zk], axis=1)
        v_all = jnp.concatenate([cache_v.reshape(B, past, ATT_DIM).astype(BF16), vb, zk], axis=1)
        tq = -(-L // LANES) * LANES
        qb = jnp.concatenate([qb, jnp.zeros((B, tq - L, ATT_QK), BF16)], axis=1)
        o_a = _attention(qb, k_all, v_all, lam, lp['subln_g'], lam_init, chunked=False, q_pos0=past,
                         n_keys=n_keys, n_q=1, tq=tq, tk=ATT_TK)
    o_a = o_a[:, :L].reshape(T, ATT_DIM)

    x1, h2, logits = _merge(o_r, o_a, pg, x2, lp, tm_small)
    dispatch = _moe_dispatch(h2, logits[:, :N_EXPERTS])
    finish = lambda: _moe_finish(x1, dispatch, moe_w).reshape(B, L, D)
    return (finish, k_rows.reshape(B, L, ATT_HEADS, 2, QK_DIM),
            pv.reshape(B, L, ATT_HEADS, V_DIM), s_fin, pr3[:, -1:])


def kernel(x_prompt, x_sample, cache_k, cache_v, state_wkv, state_shift, meta, norm1_g, w_in, shift_mu, w0, w_up, a0, a_up, g_up, k_k, k_a, r_k, lnx_g, lnx_b, q_norm_g, k_norm_g, lam_q1, lam_k1, lam_q2, lam_k2, subln_g, w_br_r, w_br_a, w_out, norm2_g, router_w, router_b, w_gu, b_gu, w_dn, b_dn):
    depth = w_in.shape[0]
    n_prompt = x_prompt.shape[0]
    meta_rows = jnp.broadcast_to(meta.astype(x_prompt.dtype)[None], (n_prompt, N_META, D_MODEL))
    xp = jnp.concatenate([meta_rows, x_prompt], axis=1)
    xs = x_sample
    st_p, st_s = [], []
    for l in range(depth):
        lp = dict(norm1_g=norm1_g[l], shift_mu=shift_mu[l], w0=w0[l], w_up=w_up[l],
                  a0=a0[l], a_up=a_up[l], g_up=g_up[l], k_k=k_k[l], k_a=k_a[l], r_k=r_k[l].reshape(-1),
                  lnx_g=lnx_g[l], lnx_b=lnx_b[l], q_norm_g=q_norm_g[l], k_norm_g=k_norm_g[l],
                  subln_g=subln_g[l], w_br_r=w_br_r[l], w_br_a=w_br_a[l], w_out=w_out[l],
                  norm2_g=norm2_g[l], router_w=router_w[l], router_b=router_b[l])
        wl = w_in[l].astype(BF16)
        c0, c1, c2 = RWKV_PROJ, RWKV_PROJ + 2 * ATT_QK, RWKV_PROJ + 2 * ATT_QK + ATT_DIM
        w_bf = (wl[:, :c0], wl[:, c0:c1], wl[:, c1:c2], wl[:, c2:])
        moe_w = (w_gu[l].astype(F32), b_gu[l].astype(F32), w_dn[l].astype(F32), b_dn[l].astype(F32))
        lam_init = 0.8 - 0.6 * math.exp(-0.3 * l)
        lam = (jnp.exp(jnp.sum(lam_q1[l].astype(F32) * lam_k1[l].astype(F32)))
               - jnp.exp(jnp.sum(lam_q2[l].astype(F32) * lam_k2[l].astype(F32))) + lam_init)
        s0 = jnp.zeros((n_prompt, RWKV_HEADS, RWKV_HEAD, RWKV_HEAD), F32)
        r0 = jnp.zeros((n_prompt, 1, RWKV_PROJ), F32)
        experts_p, kp, vp, sp, rp = _trunk_layer(xp, lp, w_bf, moe_w, lam, lam_init, s0, r0)
        experts_s, ks, vs, ss, rs = _trunk_layer(xs, lp, w_bf, moe_w, lam, lam_init, state_wkv[l],
                                                 state_shift[l], cache=(cache_k[l], cache_v[l]))
        xp, xs = experts_p(), experts_s()
        st_p.append((kp, vp, sp, rp))
        st_s.append((ks, vs, ss, rs))
    stack = lambda st, j: jnp.stack([t[j] for t in st])
    return (xp[:, N_META:], xs,
            stack(st_p, 0), stack(st_p, 1), stack(st_p, 2), stack(st_p, 3),
            stack(st_s, 0), stack(st_s, 1), stack(st_s, 2), stack(st_s, 3))
```

```python
import functools
import math

import jax
import jax.numpy as jnp
import numpy as np
from jax import lax
from jax.experimental import pallas as pl
from jax.experimental.pallas import tpu as pltpu

F32 = jnp.float32
BF16 = jnp.bfloat16

D_MODEL = 1024
CHUNK = 64
CHUNK_SHIFT = CHUNK.bit_length() - 1
assert 1 << CHUNK_SHIFT == CHUNK
N_META = 16
RWKV_HEAD = 64
RWKV_HEADS = D_MODEL // RWKV_HEAD
RWKV_DIM = RWKV_HEADS * RWKV_HEAD
DECAY_LORA = 64
AAA_LORA = 64
GATE_LORA = 128
RWKV_PROJ = 3 * RWKV_DIM + DECAY_LORA + AAA_LORA + GATE_LORA
LNX_EPS = 64e-5
ATT_HEADS = 8
QK_DIM = 64
V_DIM = 2 * QK_DIM
ATT_QK = ATT_HEADS * 2 * QK_DIM
ATT_DIM = ATT_HEADS * V_DIM
N_EXPERTS = 32
TOP_K = 4
SWIGLU_ALPHA = 1.702
SWIGLU_LIMIT = 7.0
RMS_EPS = 1e-6
NEG_INF = -1e30
LOG2E = math.log2(math.e)
Q_SCALE = QK_DIM ** -0.5 * LOG2E


def _bf16_terms(x, n):
    terms = []
    for _ in range(n):
        t = float(np.float32(x).astype(jnp.bfloat16))
        terms.append(t)
        x -= t
    return tuple(terms)


LOG2E_BF16_TERMS = _bf16_terms(LOG2E, 3)

LANES = 128
SUBLANES = 8
BF16_ROWS = 16
BF16_EXACT_INT = 256
MXU_DIM = 256
VMEM_LIMIT = 56 * 1024 * 1024

GROUP_BLOCK = MXU_DIM
MOE_BLOCK = 512
ATT_TQ = 256
ATT_TK = 512
ATT_SUB = 256
SCAN_TT = 48
SCAN_ACCUMULATORS = 4


def _cparams(*sem):
    return pltpu.CompilerParams(dimension_semantics=sem, vmem_limit_bytes=VMEM_LIMIT)


def _tile(n, target, mult):
    best = None
    for t in range(mult, min(n, target) + 1, mult):
        if n % t == 0:
            best = t
    return best if best is not None else n


def _group_ones():
    g = np.arange(GROUP_BLOCK) // RWKV_HEAD
    return jnp.asarray((g[:, None] == g[None, :]).astype(np.float32), BF16)


def _gsum(x, ones_ref):
    hi = x.astype(BF16)
    lo = (x - hi.astype(F32)).astype(BF16)
    ones = ones_ref[...]
    outs = []
    for c in range(x.shape[1] // GROUP_BLOCK):
        sl = slice(c * GROUP_BLOCK, (c + 1) * GROUP_BLOCK)
        outs.append(jnp.dot(hi[:, sl], ones, preferred_element_type=F32)
                    + jnp.dot(lo[:, sl], ones, preferred_element_type=F32))
    return jnp.concatenate(outs, axis=1)


def _norm_matmul_kernel(x_ref, g_ref, w_ref, o_ref):
    x = x_ref[...]
    h = x * lax.rsqrt(jnp.mean(x * x, axis=-1, keepdims=True) + RMS_EPS) * g_ref[...]
    o_ref[...] = jnp.dot(h.astype(BF16), w_ref[...], preferred_element_type=F32)


def _norm_matmul(x, g, w, tm):
    T, D = x.shape
    N = w.shape[1]
    return pl.pallas_call(
        _norm_matmul_kernel,
        grid=(T // tm,),
        in_specs=[pl.BlockSpec((tm, D), lambda i: (i, 0)),
                  pl.BlockSpec((1, D), lambda i: (0, 0)),
                  pl.BlockSpec((D, N), lambda i: (0, 0))],
        out_specs=pl.BlockSpec((tm, N), lambda i: (i, 0)),
        out_shape=jax.ShapeDtypeStruct((T, N), F32),
        compiler_params=_cparams("parallel"),
        name="norm_matmul",
    )(x, g.reshape(1, D), w)


def _rwkv_prep_kernel(pr_ref, prev_ref, mu_ref, w0_ref, wup_ref, a0_ref, aup_ref, gup_ref,
                      kk_ref, ka_ref, rk_ref, ones_ref,
                      kn_o, d_o, b_o, k_o, v_o, r_o, bonus_o, g_o):
    pr = pr_ref[...]
    row = lax.broadcasted_iota(jnp.int32, pr.shape, 0)
    prev = jnp.where(row == 0, prev_ref[...], pltpu.roll(pr, 1, axis=0))
    xs = pr + (prev - pr) * mu_ref[...]
    r = xs[:, :RWKV_DIM]
    k = xs[:, RWKV_DIM:2 * RWKV_DIM]
    v = xs[:, 2 * RWKV_DIM:3 * RWKV_DIM]
    lora_in = xs[:, 3 * RWKV_DIM:3 * RWKV_DIM + DECAY_LORA + AAA_LORA]
    gd = xs[:, 3 * RWKV_DIM + DECAY_LORA + AAA_LORA:]
    wl = jnp.dot(jnp.tanh(lora_in).astype(BF16), wup_ref[...], preferred_element_type=F32)
    al = jnp.dot(lora_in.astype(BF16), aup_ref[...], preferred_element_type=F32)
    z = -(w0_ref[...] + wl)
    w = -(jnp.maximum(z, 0.0) + jnp.log1p(jnp.exp(-jnp.abs(z)))) - 0.5
    decay = jnp.exp(-jnp.exp(w))
    a = jax.nn.sigmoid(a0_ref[...] + al)
    g = jnp.dot(jax.nn.sigmoid(gd).astype(BF16), gup_ref[...], preferred_element_type=F32)
    kk = k * kk_ref[...]
    nrm = jnp.sqrt(_gsum(kk * kk, ones_ref))
    kk = kk / jnp.maximum(nrm, 1e-12)
    k = k * (1.0 + (a - 1.0) * ka_ref[...])
    slab = lambda val: val.reshape(val.shape[0], RWKV_DIM // LANES, LANES)
    kn_o[...] = slab(-kk)
    d_o[...] = slab(decay)
    b_o[...] = slab(kk * a)
    k_o[...] = slab(k)
    v_o[...] = slab(v)
    r_o[...] = slab(r)
    bonus_o[...] = _gsum(r * k * rk_ref[...], ones_ref) * v
    g_o[...] = g


def _rwkv_prep(pr, prev_rows, lp, B, L, tl):
    nt = L // tl
    rows_per_token = RWKV_DIM // LANES
    row = lambda a: a.reshape(1, -1).astype(F32)
    zpad = jnp.zeros((DECAY_LORA, RWKV_DIM), F32)
    wup = jnp.concatenate([lp['w_up'], zpad], axis=0).astype(BF16)
    aup = jnp.concatenate([zpad, lp['a_up']], axis=0).astype(BF16)
    vec = lambda n: pl.BlockSpec((1, n), lambda b, i: (0, 0))
    mat = lambda m, n: pl.BlockSpec((m, n), lambda b, i: (0, 0))
    tok = pl.BlockSpec((tl, RWKV_DIM), lambda b, i: (b * nt + i, 0))
    time_major = pl.BlockSpec((tl, rows_per_token, LANES), lambda b, i: (i, b, 0))
    return pl.pallas_call(
        _rwkv_prep_kernel,
        grid=(B, nt),
        in_specs=[pl.BlockSpec((tl, RWKV_PROJ), lambda b, i: (b * nt + i, 0)),
                  pl.BlockSpec((None, 1, RWKV_PROJ), lambda b, i: (b * nt + i, 0, 0)),
                  vec(RWKV_PROJ), vec(RWKV_DIM), mat(2 * DECAY_LORA, RWKV_DIM), vec(RWKV_DIM),
                  mat(2 * AAA_LORA, RWKV_DIM), mat(GATE_LORA, RWKV_DIM),
                  vec(RWKV_DIM), vec(RWKV_DIM), vec(RWKV_DIM), mat(GROUP_BLOCK, GROUP_BLOCK)],
        out_specs=[time_major] * 6 + [tok] * 2,
        out_shape=[jax.ShapeDtypeStruct((L, B * rows_per_token, LANES), F32)] * 6
                  + [jax.ShapeDtypeStruct((B * L, RWKV_DIM), F32)] * 2,
        compiler_params=_cparams("parallel", "parallel"),
        name="rwkv_prep",
    )(pr, prev_rows, row(lp['shift_mu']), row(lp['w0']), wup, row(lp['a0']), aup,
      lp['g_up'].astype(BF16), row(lp['k_k']), row(lp['k_a']), row(lp['r_k']), _group_ones())


def _rwkv_scan_kernel(kn_ref, d_ref, b_ref, k_ref, v_ref, r_ref, s0_ref, y_ref, sfin_ref,
                      s_sc, opa_sc, opb_sc):
    tb = pl.program_id(1)

    @pl.when(tb == 0)
    def _():
        s_sc[...] = s0_ref[...]

    tt = kn_ref.shape[0]
    refs = (kn_ref, d_ref, b_ref, k_ref, v_ref, r_ref)

    def stage(t, op_sc):
        t = jnp.minimum(t, tt - 1)
        for i, ref in enumerate(refs):
            mt = ref[t].T
            op_sc[i] = jnp.concatenate([mt[:RWKV_HEAD], mt[RWKV_HEAD:]], axis=1)

    def tree_sum(terms):
        while len(terms) > 1:
            terms = [a + b for a, b in zip(terms[::2], terms[1::2])]
        return terms[0]

    class Sums:
        def __init__(self):
            self.partial = [None] * SCAN_ACCUMULATORS

        def add(self, k, term):
            a = k % SCAN_ACCUMULATORS
            self.partial[a] = term if self.partial[a] is None else self.partial[a] + term

        def total(self):
            return tree_sum(self.partial)

    def state_dot(op_sc):
        acc = Sums()
        for k in range(RWKV_HEAD):
            acc.add(k, s_sc[k] * op_sc[0, k:k + 1, :])
        return acc.total()

    def step(t, op_sc, next_sc, sa):
        row = lambda i, k: op_sc[i, k:k + 1, :]
        vv = op_sc[4]
        y_acc, sa_acc = Sums(), Sums()
        for k in range(RWKV_HEAD):
            sn = s_sc[k] * row(1, k) + sa * row(2, k) + vv * row(3, k)
            s_sc[k] = sn
            y_acc.add(k, sn * row(5, k))
            sa_acc.add(k, sn * next_sc[0, k:k + 1, :])
        y = y_acc.total()
        y_ref[t] = jnp.concatenate([y[:, :RWKV_HEAD], y[:, RWKV_HEAD:]], axis=0).T
        return sa_acc.total()

    stage(0, opa_sc)

    def pair(i, sa):
        t0 = 2 * i
        stage(t0 + 1, opb_sc)
        sa = step(t0, opa_sc, opb_sc, sa)
        stage(t0 + 2, opa_sc)
        return step(t0 + 1, opb_sc, opa_sc, sa)

    lax.fori_loop(0, tt // 2, pair, state_dot(opa_sc))

    @pl.when(tb == pl.num_programs(1) - 1)
    def _():
        sfin_ref[...] = s_sc[...]


def _scan_lane_order(nc):
    lane = np.arange(nc)
    return (lane // LANES) * LANES + 2 * (lane % RWKV_HEAD) + (lane % LANES) // RWKV_HEAD


def _rwkv_scan(ops, s0, tt):
    L, pairs, _ = ops[0].shape
    n = RWKV_HEAD
    nc = 2 * pairs
    assert tt % 2 == 0 and L % tt == 0 and nc % LANES == 0
    tspec = pl.BlockSpec((tt, n, LANES), lambda c, t: (t, c, 0))
    sspec = pl.BlockSpec((n, n, LANES), lambda c, t: (0, 0, c))
    return pl.pallas_call(
        _rwkv_scan_kernel,
        grid=(nc // LANES, L // tt),
        in_specs=[tspec] * 6 + [sspec],
        out_specs=[tspec, sspec],
        out_shape=[jax.ShapeDtypeStruct((L, pairs, LANES), F32), jax.ShapeDtypeStruct((n, n, nc), F32)],
        scratch_shapes=[pltpu.VMEM((n, n, LANES), F32), pltpu.VMEM((6, n, LANES), F32),
                        pltpu.VMEM((6, n, LANES), F32)],
        compiler_params=_cparams("parallel", "arbitrary"),
        name="rwkv_scan",
    )(*ops, s0)


def _rwkv_post_kernel(y_ref, bonus_ref, g_ref, lg_ref, lb_ref, ones_ref, o_ref):
    y = jnp.concatenate([y_ref[:, j, :] for j in range(y_ref.shape[1])], axis=1)
    inv_n = 1.0 / RWKV_HEAD
    mean = _gsum(y, ones_ref) * inv_n
    yc = y - mean
    var = _gsum(yc * yc, ones_ref) * inv_n
    yn = yc * lax.rsqrt(var + LNX_EPS) * lg_ref[...] + lb_ref[...]
    o_ref[...] = ((yn + bonus_ref[...]) * g_ref[...]).astype(BF16)


def _rwkv_post(y, bonus, g, lp, B, L, tl):
    nt = L // tl
    tok = pl.BlockSpec((tl, RWKV_DIM), lambda b, i: (b * nt + i, 0))
    vec = pl.BlockSpec((1, RWKV_DIM), lambda b, i: (0, 0))
    return pl.pallas_call(
        _rwkv_post_kernel,
        grid=(B, nt),
        in_specs=[pl.BlockSpec((tl, RWKV_DIM // LANES, LANES), lambda b, i: (i, b, 0)), tok, tok, vec, vec,
                  pl.BlockSpec((GROUP_BLOCK, GROUP_BLOCK), lambda b, i: (0, 0))],
        out_specs=tok,
        out_shape=jax.ShapeDtypeStruct((B * L, RWKV_DIM), BF16),
        compiler_params=_cparams("parallel", "parallel"),
        name="rwkv_post",
    )(y, bonus, g, lp['lnx_g'].reshape(1, -1), lp['lnx_b'].reshape(1, -1), _group_ones())


def _qk_norm_kernel(qk_ref, v_ref, qg_ref, kg_ref, ones_ref, kout_ref, qb_ref, kb_ref, vb_ref, *, length):
    tl = qk_ref.shape[0]
    valid = (pl.program_id(1) * tl + lax.broadcasted_iota(jnp.int32, (tl, 1), 0)) < length
    q = qk_ref[:, :ATT_QK]
    k = qk_ref[:, ATT_QK:]
    inv_n = 1.0 / QK_DIM
    qn = q * lax.rsqrt(_gsum(q * q, ones_ref) * inv_n + RMS_EPS) * qg_ref[...]
    kn = k * lax.rsqrt(_gsum(k * k, ones_ref) * inv_n + RMS_EPS) * kg_ref[...]
    kout_ref[...] = kn
    qb_ref[...] = jnp.where(valid, qn * Q_SCALE, 0.0).astype(BF16)
    kb_ref[...] = jnp.where(valid, kn, 0.0).astype(BF16)
    vb_ref[...] = jnp.where(valid, v_ref[...], 0.0).astype(BF16)


def _qk_norm(pqk, pv, lp, lpad, tl):
    B, L, _ = pqk.shape
    tile_g = lambda g: jnp.tile(g.astype(F32), ATT_QK // QK_DIM).reshape(1, ATT_QK)
    blk = lambda n: pl.BlockSpec((None, tl, n), lambda b, i: (b, i, 0))
    vec = pl.BlockSpec((1, ATT_QK), lambda b, i: (0, 0))
    bshape = jax.ShapeDtypeStruct((B, lpad, ATT_QK), BF16)
    return pl.pallas_call(
        functools.partial(_qk_norm_kernel, length=L),
        grid=(B, lpad // tl),
        in_specs=[blk(2 * ATT_QK), blk(ATT_DIM), vec, vec,
                  pl.BlockSpec((GROUP_BLOCK, GROUP_BLOCK), lambda b, i: (0, 0))],
        out_specs=[blk(ATT_QK)] * 4,
        out_shape=[jax.ShapeDtypeStruct((B, L, ATT_QK), F32), bshape, bshape, bshape],
        compiler_params=_cparams("parallel", "parallel"),
        name="qk_norm",
    )(pqk, pv, tile_g(lp['q_norm_g']), tile_g(lp['k_norm_g']), _group_ones())


def _attn_kernel(slope_ref, lam_ref, q_ref, k_ref, kpos_ref, v_ref, g_ref, o_ref,
                 qt_sc, vt_sc, pen_sc, sa_sc, sb_sc, m_sc, a_sc,
                 *, chunked, q_pos0, n_keys, out_scale):
    tq = q_ref.shape[0]
    n_tiles, _, tk = vt_sc.shape
    h = pl.program_id(1)
    i = pl.program_id(2)

    @pl.when(i == 0)
    def _():
        ones_rows = (lax.broadcasted_iota(jnp.int32, (BF16_ROWS, tk), 0) == 0).astype(F32).astype(BF16)

        def transpose_tile(j, carry):
            k0 = pl.multiple_of(j * tk, tk)
            vt_sc[j, :V_DIM, :] = v_ref[pl.ds(k0, tk), :].astype(F32).T.astype(BF16)
            vt_sc[j, V_DIM:, :] = ones_rows
            return carry
        lax.fori_loop(0, n_tiles, transpose_tile, 0)

    qt = q_ref[...].astype(F32).T
    sub = lax.broadcasted_iota(jnp.int32, qt.shape, 0)
    slope = slope_ref[h]
    coef = jnp.zeros(qt.shape, F32)
    for r, c in enumerate([w * c for w in (CHUNK, 1) for c in LOG2E_BF16_TERMS]):
        coef = jnp.where(sub == r, slope * c, coef)
    coef = coef.astype(BF16)
    for mi, keep in enumerate((sub < QK_DIM, sub >= QK_DIM)):
        qt_sc[mi, :2 * QK_DIM, :] = jnp.where(keep, qt, 0.0).astype(BF16)
        qt_sc[mi, 2 * QK_DIM:, :] = coef
    q_first = q_pos0 + i * tq
    qpos = q_first + lax.broadcasted_iota(jnp.int32, (1, tq), 1)
    if chunked:
        chunk_end = lambda p: N_META + CHUNK + (((p - N_META) >> CHUNK_SHIFT) << CHUNK_SHIFT)
        vis = jnp.where(qpos < N_META, N_META, chunk_end(qpos))
        vis_last = chunk_end(q_first + tq - 1)
        n_kt = (jnp.minimum(vis_last, n_keys) + tk - 1) // tk
    else:
        vis = jnp.full((1, tq), n_keys, jnp.int32)
        n_kt = (n_keys + tk - 1) // tk
    n_past = jnp.minimum((q_first + 1) // tk, n_kt)

    m_sc[...] = jnp.full(m_sc.shape, -jnp.inf, F32)
    a_sc[...] = jnp.zeros(a_sc.shape, F32)

    def accumulate(mi, s, vtj):
        m_old = m_sc[mi]
        m_new = jnp.maximum(m_old, jnp.max(s, axis=0, keepdims=True))
        p = jnp.exp2(s - m_new)
        a_sc[mi] = jnp.exp2(m_old - m_new) * a_sc[mi] + jnp.dot(vtj, p.astype(BF16),
                                                                preferred_element_type=F32)
        m_sc[mi] = m_new

    def scores_into(s_ref, j):
        k0 = pl.multiple_of(jnp.minimum(j, n_tiles - 1) * tk, tk)
        kj = jnp.concatenate([k_ref[pl.ds(k0, tk), :], kpos_ref[pl.ds(k0, tk), :]], axis=1)
        for mi in range(2):
            s_ref[mi] = jnp.dot(kj, qt_sc[mi], preferred_element_type=F32)

    def consume(s_ref, j, pen_ref=None):
        for k0 in range(0, tk, ATT_SUB):
            for mi in range(2):
                s = s_ref[mi, k0:k0 + ATT_SUB, :]
                if pen_ref is not None:
                    s = s - pen_ref[k0:k0 + ATT_SUB, :]
                accumulate(mi, s, vt_sc[j, :, k0:k0 + ATT_SUB])

    n_pairs = n_past // 2

    @pl.when(n_pairs > 0)
    def _():
        scores_into(sa_sc, 0)

    def pair_body(jj, carry):
        j0 = 2 * jj
        scores_into(sb_sc, j0 + 1)
        consume(sa_sc, j0)
        scores_into(sa_sc, j0 + 2)
        consume(sb_sc, j0 + 1)
        return carry

    lax.fori_loop(0, n_pairs, pair_body, 0)

    def past_single(j, carry):
        scores_into(sa_sc, j)
        consume(sa_sc, j)
        return carry

    lax.fori_loop(2 * n_pairs, n_past, past_single, 0)

    def edge_body(j, carry):
        kpos = j * tk + lax.broadcasted_iota(jnp.int32, (tk, 1), 0)
        ahead = jnp.maximum(kpos - qpos, 0).astype(F32)
        pen_sc[...] = jnp.where(kpos < vis, (2.0 * LOG2E * slope) * ahead, -NEG_INF)
        scores_into(sa_sc, j)
        consume(sa_sc, j, pen_sc)
        return carry

    lax.fori_loop(n_past, n_kt, edge_body, 0)

    num = [a_sc[mi, :V_DIM, :] / a_sc[mi, V_DIM:V_DIM + 1, :] for mi in range(2)]
    o = num[0] - lam_ref[0] * num[1]
    o = o * lax.rsqrt(jnp.mean(o * o, axis=0, keepdims=True) + RMS_EPS) * g_ref[...] * out_scale
    o_ref[...] = o.T.astype(BF16)


def _attention(qb, kb, vb, lam, subln_g, lam_init, *, chunked, q_pos0, n_keys, n_q, tq, tk):
    B = qb.shape[0]
    lk = kb.shape[1]
    slopes = jnp.asarray([2.0 ** (-8.0 * (i + 1) / ATT_HEADS) for i in range(ATT_HEADS)], F32)
    gain = jnp.broadcast_to(subln_g.astype(F32).reshape(V_DIM, 1), (V_DIM, tq))
    assert lk // CHUNK <= BF16_EXACT_INT
    kpos = np.arange(lk)
    digits = np.zeros((lk, V_DIM), np.float32)
    digits[:, 0:3] = (kpos // CHUNK)[:, None]
    digits[:, 3:6] = (kpos % CHUNK)[:, None]
    kpos_digits = jnp.asarray(digits, BF16)
    smem = pl.BlockSpec(memory_space=pltpu.SMEM)
    qspec = pl.BlockSpec((None, tq, V_DIM), lambda b, h, i: (b, i, h))
    kspec = pl.BlockSpec((None, lk, V_DIM), lambda b, h, i: (b, 0, h))
    const = lambda shape: pl.BlockSpec(shape, lambda b, h, i: (0, 0))
    acc_rows = V_DIM + BF16_ROWS
    return pl.pallas_call(
        functools.partial(_attn_kernel, chunked=chunked, q_pos0=q_pos0, n_keys=n_keys,
                          out_scale=1.0 - lam_init),
        grid=(B, ATT_HEADS, n_q),
        in_specs=[smem, smem, qspec, kspec, const((lk, V_DIM)), kspec, const((V_DIM, tq))],
        out_specs=qspec,
        out_shape=jax.ShapeDtypeStruct((B, n_q * tq, ATT_DIM), BF16),
        scratch_shapes=[pltpu.VMEM((2, 2 * V_DIM, tq), BF16), pltpu.VMEM((lk // tk, acc_rows, tk), BF16),
                        pltpu.VMEM((tk, tq), F32),
                        pltpu.VMEM((2, tk, tq), F32), pltpu.VMEM((2, tk, tq), F32),
                        pltpu.VMEM((2, 1, tq), F32),
                        pltpu.VMEM((2, acc_rows, tq), F32)],
        compiler_params=_cparams("parallel", "parallel", "arbitrary"),
        name="diff_attention",
    )(slopes, lam.reshape(1).astype(F32), qb, kb, kpos_digits, vb, gain)


def _merge_kernel(or_ref, oa_ref, pg_ref, x_ref, wr_ref, wa_ref, wo_ref, g2_ref, rw_ref, rb_ref,
                  x1_ref, h2_ref, lg_ref):
    gates = jax.nn.sigmoid(pg_ref[...])
    br = jnp.dot(or_ref[...], wr_ref[...], preferred_element_type=F32)
    ba = jnp.dot(oa_ref[...], wa_ref[...], preferred_element_type=F32)
    merged = gates[:, :D_MODEL] * br + gates[:, D_MODEL:] * ba
    x1 = x_ref[...] + jnp.dot(merged.astype(BF16), wo_ref[...], preferred_element_type=F32)
    h2 = x1 * lax.rsqrt(jnp.mean(x1 * x1, axis=-1, keepdims=True) + RMS_EPS) * g2_ref[...]
    x1_ref[...] = x1
    h2_ref[...] = h2.astype(BF16)
    h_hi = h2.astype(BF16)
    h_lo = (h2 - h_hi.astype(F32)).astype(BF16)
    w_hi, w_lo = rw_ref[0], rw_ref[1]
    lg_ref[...] = (jnp.dot(h_hi, w_hi, preferred_element_type=F32)
                   + jnp.dot(h_hi, w_lo, preferred_element_type=F32)
                   + jnp.dot(h_lo, w_hi, preferred_element_type=F32)) + rb_ref[...]


def _merge(o_r, o_a, pg, x, lp, tm):
    T = x.shape[0]
    rw = jnp.zeros((D_MODEL, LANES), F32).at[:, :N_EXPERTS].set(lp['router_w'].astype(F32))
    rw_hi = rw.astype(BF16)
    rw = jnp.stack([rw_hi, (rw - rw_hi.astype(F32)).astype(BF16)])
    rb = jnp.zeros((1, LANES), F32).at[0, :N_EXPERTS].set(lp['router_b'].astype(F32))
    tok = lambda n: pl.BlockSpec((tm, n), lambda i: (i, 0))
    mat = lambda m, n: pl.BlockSpec((m, n), lambda i: (0, 0))
    return pl.pallas_call(
        _merge_kernel,
        grid=(T // tm,),
        in_specs=[tok(RWKV_DIM), tok(ATT_DIM), tok(2 * D_MODEL), tok(D_MODEL),
                  mat(RWKV_DIM, D_MODEL), mat(ATT_DIM, D_MODEL), mat(D_MODEL, D_MODEL),
                  mat(1, D_MODEL), pl.BlockSpec((2, D_MODEL, LANES), lambda i: (0, 0, 0)), mat(1, LANES)],
        out_specs=[tok(D_MODEL), tok(D_MODEL), tok(LANES)],
        out_shape=[jax.ShapeDtypeStruct((T, D_MODEL), F32), jax.ShapeDtypeStruct((T, D_MODEL), BF16),
                   jax.ShapeDtypeStruct((T, LANES), F32)],
        compiler_params=_cparams("parallel"),
        name="merge_router",
    )(o_r, o_a, pg, x, lp['w_br_r'].astype(BF16), lp['w_br_a'].astype(BF16), lp['w_out'].astype(BF16),
      lp['norm2_g'].reshape(1, -1), rw, rb)


def _moe_kernel(be_ref, nact_ref, x_ref, gate_ref, wgu_ref, bgu_ref, wdn_ref, bdn_ref, o_ref,
                wgu_sc, wdn_sc):
    i = pl.program_id(0)
    active = i < nact_ref[0]
    new_expert = jnp.logical_or(i == 0, be_ref[i] != be_ref[jnp.maximum(i - 1, 0)])

    @pl.when(jnp.logical_and(active, new_expert))
    def _():
        for dst, src in ((wgu_sc, wgu_ref), (wdn_sc, wdn_ref)):
            for r0 in range(0, src.shape[0], MXU_DIM):
                dst[r0:r0 + MXU_DIM, :] = src[r0:r0 + MXU_DIM, :].astype(BF16)

    @pl.when(active)
    def _():
        gu = jnp.dot(x_ref[...], wgu_sc[...], preferred_element_type=F32) + bgu_ref[...]
        d_ff = gu.shape[1] // 2
        x_glu = jnp.minimum(gu[:, :d_ff], SWIGLU_LIMIT)
        x_lin = jnp.clip(gu[:, d_ff:], -SWIGLU_LIMIT, SWIGLU_LIMIT)
        act = x_glu * jax.nn.sigmoid(SWIGLU_ALPHA * x_glu) * (x_lin + 1.0)
        out = jnp.dot(act.astype(BF16), wdn_sc[...], preferred_element_type=F32) + bdn_ref[...]
        o_ref[...] = out * gate_ref[...]

    @pl.when(i >= nact_ref[0])
    def _():
        o_ref[...] = jnp.zeros_like(o_ref)


def _moe_ffn_blocks(xs, row_gate, blk_e, n_act, w_gu, b_gu, w_dn, b_dn):
    n_rows, D = xs.shape
    n_blk = n_rows // MOE_BLOCK
    d_ff2 = w_gu.shape[2]
    grid_spec = pltpu.PrefetchScalarGridSpec(
        num_scalar_prefetch=2,
        grid=(n_blk,),
        in_specs=[pl.BlockSpec((MOE_BLOCK, D), lambda i, be, na: (i, 0)),
                  pl.BlockSpec((MOE_BLOCK, 1), lambda i, be, na: (i, 0)),
                  pl.BlockSpec((None, D, d_ff2), lambda i, be, na: (be[i], 0, 0)),
                  pl.BlockSpec((None, 1, d_ff2), lambda i, be, na: (be[i], 0, 0)),
                  pl.BlockSpec((None, d_ff2 // 2, D), lambda i, be, na: (be[i], 0, 0)),
                  pl.BlockSpec((None, 1, D), lambda i, be, na: (be[i], 0, 0))],
        out_specs=pl.BlockSpec((MOE_BLOCK, D), lambda i, be, na: (i, 0)),
        scratch_shapes=[pltpu.VMEM((D, d_ff2), BF16), pltpu.VMEM((d_ff2 // 2, D), BF16)],
    )
    return pl.pallas_call(
        _moe_kernel,
        grid_spec=grid_spec,
        out_shape=jax.ShapeDtypeStruct((n_rows, D), F32),
        compiler_params=_cparams("arbitrary"),
        name="moe_ffn",
    )(blk_e, n_act, xs, row_gate.reshape(n_rows, 1), w_gu, b_gu.reshape(N_EXPERTS, 1, d_ff2),
      w_dn, b_dn.reshape(N_EXPERTS, 1, D))


def _moe_dispatch(h2, logits):
    T, D = h2.shape
    top_v, top_e = lax.top_k(logits, TOP_K)
    gate = jax.nn.softmax(top_v, axis=-1)
    rows = T * TOP_K
    flat_e = top_e.reshape(-1).astype(jnp.int32)
    e_sorted, order = lax.sort((flat_e, jnp.arange(rows, dtype=jnp.int32)), num_keys=1, is_stable=True)
    rank = jnp.argsort(order).astype(jnp.int32)
    bounds = jnp.searchsorted(e_sorted, jnp.arange(N_EXPERTS + 1, dtype=jnp.int32)).astype(jnp.int32)
    start, counts = bounds[:-1], bounds[1:] - bounds[:-1]
    padded = (counts + MOE_BLOCK - 1) // MOE_BLOCK * MOE_BLOCK
    pad_end = jnp.cumsum(padded)
    pad_start = pad_end - padded
    n_blk = -(-(rows + N_EXPERTS * (MOE_BLOCK - 1)) // MOE_BLOCK)
    n_rows = n_blk * MOE_BLOCK
    blk_start = jnp.arange(n_blk, dtype=jnp.int32) * MOE_BLOCK
    blk_e = jnp.minimum(jnp.sum(blk_start[:, None] >= pad_end[None, :], axis=1, dtype=jnp.int32), N_EXPERTS - 1)
    n_act = (pad_end[-1:] // MOE_BLOCK).astype(jnp.int32)
    row_off = (blk_start - pad_start[blk_e])[:, None] + jnp.arange(MOE_BLOCK, dtype=jnp.int32)[None, :]
    row_valid = row_off < counts[blk_e][:, None]
    row_flat = order[jnp.where(row_valid, start[blk_e][:, None] + row_off, 0)]
    spread = (blk_start[:, None] + jnp.arange(MOE_BLOCK, dtype=jnp.int32)[None, :]) % T
    row_tok = jnp.where(row_valid, row_flat // TOP_K, spread).reshape(n_rows)
    row_gate = jnp.where(row_valid, gate.reshape(-1)[row_flat], 0.0).reshape(n_rows)
    pos = (pad_start[flat_e] + rank - start[flat_e]).reshape(T, TOP_K)
    return h2[row_tok], row_gate, blk_e, n_act, pos


def _moe_finish(x1, dispatch, moe_w):
    xs, row_gate, blk_e, n_act, pos = dispatch
    out = _moe_ffn_blocks(xs, row_gate, blk_e, n_act, *moe_w)
    y = x1
    for k in range(TOP_K):
        y = y + out[pos[:, k]]
    return y


def _trunk_layer(x, lp, w_bf, moe_w, lam, lam_init, s0, prev_row, cache=None):
    B, L, D = x.shape
    T = B * L
    x2 = x.reshape(T, D)
    tl = _tile(L, 320, BF16_ROWS)
    tm = _tile(T, 1024, BF16_ROWS) if cache is not None else _tile(L, 1024, BF16_ROWS)
    tm_small = _tile(tm, 320, BF16_ROWS)
    w_r, w_qk, w_v, w_g = w_bf

    pr = _norm_matmul(x2, lp['norm1_g'], w_r, tm_small)
    pqk = _norm_matmul(x2, lp['norm1_g'], w_qk, tm)
    pv = _norm_matmul(x2, lp['norm1_g'], w_v, tm)
    pg = _norm_matmul(x2, lp['norm1_g'], w_g, tm)

    pr3 = pr.reshape(B, L, RWKV_PROJ)
    tile_last = pr3.reshape(B, L // tl, tl, RWKV_PROJ)[:, :-1, -1]
    prev_rows = jnp.concatenate([prev_row.astype(F32), tile_last], axis=1).reshape(T // tl, 1, RWKV_PROJ)
    *scan_ops, bonus, g = _rwkv_prep(pr, prev_rows, lp, B, L, tl)
    nc = B * RWKV_HEADS
    lane_chain = _scan_lane_order(nc)
    s0c = s0.astype(F32).transpose(3, 2, 0, 1).reshape(RWKV_HEAD, RWKV_HEAD, nc)[:, :, lane_chain]
    y, s_fin_c = _rwkv_scan(scan_ops, s0c, _tile(L, SCAN_TT, 2))
    s_fin_c = s_fin_c[:, :, np.argsort(lane_chain)]
    s_fin = s_fin_c.reshape(RWKV_HEAD, RWKV_HEAD, B, RWKV_HEADS).transpose(2, 3, 1, 0)
    o_r = _rwkv_post(y, bonus, g, lp, B, L, tl)

    if cache is None:
        assert ATT_TK % ATT_TQ == 0
        lpad = -(-L // ATT_TK) * ATT_TK
        k_rows, qb, kb, vb = _qk_norm(pqk.reshape(B, L, 2 * ATT_QK), pv.reshape(B, L, ATT_DIM), lp, lpad,
                                      _tile(lpad, 512, BF16_ROWS))
        o_a = _attention(qb, kb, vb, lam, lp['subln_g'], lam_init, chunked=True, q_pos0=0, n_keys=L,
                         n_q=-(-L // ATT_TQ), tq=ATT_TQ, tk=ATT_TK)
    else:
        cache_k, cache_v = cache
        past = cache_k.shape[1]
        k_rows, qb, kb, vb = _qk_norm(pqk.reshape(B, L, 2 * ATT_QK), pv.reshape(B, L, ATT_DIM), lp, L, L)
        n_keys = past + L
        lk_pad = -(-n_keys // ATT_TK) * ATT_TK
        zk = jnp.zeros((B, lk_pad - n_keys, ATT_QK), BF16)
        k_all = jnp.concatenate([cache_k.reshape(B, past, ATT_QK).astype(BF16), kb, zk], axis=1)
        v_all = jnp.concatenate([cache_v.reshape(B, past, ATT_DIM).astype(BF16), vb, zk], axis=1)
        tq = -(-L // LANES) * LANES
        qb = jnp.concatenate([qb, jnp.zeros((B, tq - L, ATT_QK), BF16)], axis=1)
        o_a = _attention(qb, k_all, v_all, lam, lp['subln_g'], lam_init, chunked=False, q_pos0=past,
                         n_keys=n_keys, n_q=1, tq=tq, tk=ATT_TK)
    o_a = o_a[:, :L].reshape(T, ATT_DIM)

    x1, h2, logits = _merge(o_r, o_a, pg, x2, lp, tm_small)
    dispatch = _moe_dispatch(h2, logits[:, :N_EXPERTS])
    finish = lambda: _moe_finish(x1, dispatch, moe_w).reshape(B, L, D)
    return (finish, k_rows.reshape(B, L, ATT_HEADS, 2, QK_DIM),
            pv.reshape(B, L, ATT_HEADS, V_DIM), s_fin, pr3[:, -1:])


def kernel(x_prompt, x_sample, cache_k, cache_v, state_wkv, state_shift, meta, norm1_g, w_in, shift_mu, w0, w_up, a0, a_up, g_up, k_k, k_a, r_k, lnx_g, lnx_b, q_norm_g, k_norm_g, lam_q1, lam_k1, lam_q2, lam_k2, subln_g, w_br_r, w_br_a, w_out, norm2_g, router_w, router_b, w_gu, b_gu, w_dn, b_dn):
    depth = w_in.shape[0]
    n_prompt = x_prompt.shape[0]
    meta_rows = jnp.broadcast_to(meta.astype(x_prompt.dtype)[None], (n_prompt, N_META, D_MODEL))
    xp = jnp.concatenate([meta_rows, x_prompt], axis=1)
    xs = x_sample
    st_p, st_s = [], []
    for l in range(depth):
        lp = dict(norm1_g=norm1_g[l], shift_mu=shift_mu[l], w0=w0[l], w_up=w_up[l],
                  a0=a0[l], a_up=a_up[l], g_up=g_up[l], k_k=k_k[l], k_a=k_a[l], r_k=r_k[l].reshape(-1),
                  lnx_g=lnx_g[l], lnx_b=lnx_b[l], q_norm_g=q_norm_g[l], k_norm_g=k_norm_g[l],
                  subln_g=subln_g[l], w_br_r=w_br_r[l], w_br_a=w_br_a[l], w_out=w_out[l],
                  norm2_g=norm2_g[l], router_w=router_w[l], router_b=router_b[l])
        wl = w_in[l].astype(BF16)
        c0, c1, c2 = RWKV_PROJ, RWKV_PROJ + 2 * ATT_QK, RWKV_PROJ + 2 * ATT_QK + ATT_DIM
        w_bf = (wl[:, :c0], wl[:, c0:c1], wl[:, c1:c2], wl[:, c2:])
        moe_w = (w_gu[l].astype(F32), b_gu[l].astype(F32), w_dn[l].astype(F32), b_dn[l].astype(F32))
        lam_init = 0.8 - 0.6 * math.exp(-0.3 * l)
        lam = (jnp.exp(jnp.sum(lam_q1[l].astype(F32) * lam_k1[l].astype(F32)))
               - jnp.exp(jnp.sum(lam_q2[l].astype(F32) * lam_k2[l].astype(F32))) + lam_init)
        s0 = jnp.zeros((n_prompt, RWKV_HEADS, RWKV_HEAD, RWKV_HEAD), F32)
        r0 = jnp.zeros((n_prompt, 1, RWKV_PROJ), F32)
        experts_p, kp, vp, sp, rp = _trunk_layer(xp, lp, w_bf, moe_w, lam, lam_init, s0, r0)
        experts_s, ks, vs, ss, rs = _trunk_layer(xs, lp, w_bf, moe_w, lam, lam_init, state_wkv[l],
                                                 state_shift[l], cache=(cache_k[l], cache_v[l]))
        xp, xs = experts_p(), experts_s()
        st_p.append((kp, vp, sp, rp))
        st_s.append((ks, vs, ss, rs))
    stack = lambda st, j: jnp.stack([t[j] for t in st])
    return (xp[:, N_META:], xs,
            stack(st_p, 0), stack(st_p, 1), stack(st_p, 2), stack(st_p, 3),
            stack(st_s, 0), stack(st_s, 1), stack(st_s, 2), stack(st_s, 3))
```
